```python
import jax, jax.numpy as jnp
from jax import lax
import numpy as np

D_MODEL = 2048
BATCH = 2
SEQ = 8192
DEPTH = 1

MEM_LEN = 256
MIX_WIDTH = D_MODEL
CONV_CH = D_MODEL // 2
CONV_WIDTH = 31
N_HEADS = 16
HEAD_DIM = (MIX_WIDTH - CONV_CH) // N_HEADS
N_KV_HEADS = 4
GQA_GROUP = N_HEADS // N_KV_HEADS
KV_WIDTH = N_KV_HEADS * HEAD_DIM
CMP_LEN = 32
CMP_STRIDE = 16
CMP_HIDDEN = 256
SEL_LEN = 64
SEL_TOPK = 16
WIN = 512
Q_BLOCK = 128
FORCE_SCORE = 1.0e4
N_XHEADS = 4
XHEAD_DIM = D_MODEL // 16
N_EXPERTS = 32
TOP_K = 4
D_FF = D_MODEL
SWIGLU_LIMIT = 7.0
SWIGLU_ALPHA = 1.702
EXPERT_BLOCK = 128
NORM_EPS = 1e-5
NEG = -1e30
N_IN = 2 * CONV_CH + N_HEADS * HEAD_DIM + 6 * KV_WIDTH + 3 * N_HEADS

kernel_name = 'hybrid_conformer_nsa_moe_block'


def rmsnorm(x, g):
    xf = x.astype(jnp.float32)
    y = xf * lax.rsqrt(jnp.mean(xf * xf, axis=-1, keepdims=True) + NORM_EPS)
    return (y * g.astype(jnp.float32)).astype(x.dtype)


def masked_softmax(s, mask):
    s = jnp.where(mask, s.astype(jnp.float32), NEG)
    m = jnp.max(s, axis=-1, keepdims=True)
    e = jnp.where(mask, jnp.exp(s - m), 0.0)
    d = jnp.sum(e, axis=-1, keepdims=True)
    return e / jnp.where(d > 0, d, 1.0)


def alibi_slopes():
    sl = 2.0 ** (-8.0 * np.arange(1, N_HEADS + 1) / N_HEADS)
    return jnp.asarray(sl, jnp.float32).reshape(N_KV_HEADS, GQA_GROUP)


def conv_module(u, w, b, ln_g, ln_b):
    a = u[..., :CONV_CH] * jax.nn.sigmoid(u[..., CONV_CH:])
    y = lax.conv_general_dilated(a, w[:, None, :].astype(a.dtype), window_strides=(1,),
                                 padding=[(CONV_WIDTH - 1, 0)],
                                 dimension_numbers=('NWC', 'WIO', 'NWC'),
                                 feature_group_count=CONV_CH) + b
    yf = y.astype(jnp.float32)
    mu = jnp.mean(yf, axis=-1, keepdims=True)
    var = jnp.mean(jnp.square(yf - mu), axis=-1, keepdims=True)
    yn = (yf - mu) * lax.rsqrt(var + NORM_EPS) * ln_g.astype(jnp.float32) + ln_b.astype(jnp.float32)
    return jax.nn.silu(yn).astype(u.dtype)


def compress(k_raw, pe, w1, b1, w2, b2):
    B, S = k_raw.shape[:2]
    n_cmp = (S - CMP_LEN) // CMP_STRIDE + 1
    idx = np.arange(n_cmp)[:, None] * CMP_STRIDE + np.arange(CMP_LEN)[None, :]
    blk = k_raw[:, idx] + pe[:, None, :]
    flat = jnp.transpose(blk, (0, 1, 3, 2, 4)).reshape(B, n_cmp, N_KV_HEADS, CMP_LEN * HEAD_DIM)
    return jax.nn.silu(flat @ w1 + b1) @ w2 + b2


def selection_overlap(n_cmp, n_sel):
    cs = np.arange(n_cmp) * CMP_STRIDE
    ce = cs + CMP_LEN - 1
    js = np.arange(n_sel) * SEL_LEN
    m = (cs[:, None] <= js[None, :] + SEL_LEN - 1) & (ce[:, None] >= js[None, :])
    return m.astype(np.float32)


def nsa_mixer(q, k_c, v_c, k_s, v_s, k_w, v_w, gate_logits, pe, w1, b1, w2, b2):
    B, S = q.shape[:2]
    qh = (q * (HEAD_DIM ** -0.5)).reshape(B, S, N_KV_HEADS, GQA_GROUP, HEAD_DIM)
    kc = compress(k_c, pe[0], w1[0], b1[0], w2[0], b2[0])
    vc = compress(v_c, pe[1], w1[1], b1[1], w2[1], b2[1])
    n_cmp = kc.shape[1]
    cmp_end = jnp.arange(n_cmp) * CMP_STRIDE + CMP_LEN - 1
    n_sel = S // SEL_LEN
    n_top = min(SEL_TOPK, n_sel)
    overlap = jnp.asarray(selection_overlap(n_cmp, n_sel))
    ks_blk = k_s.reshape(B, n_sel, SEL_LEN, N_KV_HEADS, HEAD_DIM).transpose(0, 3, 1, 2, 4)
    vs_blk = v_s.reshape(B, n_sel, SEL_LEN, N_KV_HEADS, HEAD_DIM).transpose(0, 3, 1, 2, 4)
    kw_pad = jnp.pad(k_w, ((0, 0), (WIN, 0), (0, 0), (0, 0)))
    vw_pad = jnp.pad(v_w, ((0, 0), (WIN, 0), (0, 0), (0, 0)))
    gates = jax.nn.sigmoid(gate_logits.astype(jnp.float32)).reshape(B, S, 3, N_KV_HEADS, GQA_GROUP)
    slopes = alibi_slopes()
    b_ix = jnp.arange(B)[:, None, None, None]
    h_ix = jnp.arange(N_KV_HEADS)[None, :, None, None]
    sel_j = jnp.arange(n_sel)

    def q_block(i):
        q0 = i * Q_BLOCK
        qb = lax.dynamic_slice_in_dim(qh, q0, Q_BLOCK, axis=1)
        t = q0 + jnp.arange(Q_BLOCK)
        dist_c = t[:, None] - cmp_end[None, :]
        s_c = jnp.einsum('bqkgd,bckd->bkgqc', qb, kc).astype(jnp.float32) \
            - slopes[:, :, None, None] * dist_c.astype(jnp.float32)
        p_c = masked_softmax(s_c, dist_c >= 0)
        o_c = jnp.einsum('bkgqc,bckd->bqkgd', p_c.astype(vc.dtype), vc)
        imp = jnp.einsum('bkgqc,cj->bkqj', p_c, overlap)
        blk_t = t // SEL_LEN
        forced = (sel_j[None, :] == 0) | (sel_j[None, :] == blk_t[:, None]) | (sel_j[None, :] == blk_t[:, None] - 1)
        causal = sel_j[None, :] <= blk_t[:, None]
        score = jnp.where(forced, FORCE_SCORE, jnp.where(causal, imp, -FORCE_SCORE))
        top_v, top_j = lax.top_k(score, n_top)
        blk_ok = top_v > -0.5 * FORCE_SCORE
        ksb = ks_blk[b_ix, h_ix, top_j].reshape(B, N_KV_HEADS, Q_BLOCK, n_top * SEL_LEN, HEAD_DIM)
        vsb = vs_blk[b_ix, h_ix, top_j].reshape(B, N_KV_HEADS, Q_BLOCK, n_top * SEL_LEN, HEAD_DIM)
        pos = (top_j[..., None] * SEL_LEN + jnp.arange(SEL_LEN)).reshape(B, N_KV_HEADS, Q_BLOCK, n_top * SEL_LEN)
        dist_s = t[None, None, :, None] - pos
        mask_s = jnp.repeat(blk_ok, SEL_LEN, axis=-1) & (dist_s >= 0)
        s_s = jnp.einsum('bqkgd,bkqnd->bkgqn', qb, ksb).astype(jnp.float32) \
            - slopes[None, :, :, None, None] * dist_s[:, :, None].astype(jnp.float32)
        p_s = masked_softmax(s_s, mask_s[:, :, None])
        o_s = jnp.einsum('bkgqn,bkqnd->bqkgd', p_s.astype(vsb.dtype), vsb)
        kw = lax.dynamic_slice_in_dim(kw_pad, q0, Q_BLOCK + WIN, axis=1)
        vw = lax.dynamic_slice_in_dim(vw_pad, q0, Q_BLOCK + WIN, axis=1)
        spos = q0 - WIN + jnp.arange(Q_BLOCK + WIN)
        dist_w = t[:, None] - spos[None, :]
        mask_w = (dist_w >= 0) & (dist_w < WIN) & (spos[None, :] >= 0)
        s_w = jnp.einsum('bqkgd,bskd->bkgqs', qb, kw).astype(jnp.float32) \
            - slopes[:, :, None, None] * dist_w.astype(jnp.float32)
        p_w = masked_softmax(s_w, mask_w)
        o_w = jnp.einsum('bkgqs,bskd->bqkgd', p_w.astype(vw.dtype), vw)
        g = lax.dynamic_slice_in_dim(gates, q0, Q_BLOCK, axis=1)
        o = g[:, :, 0, :, :, None] * o_c + g[:, :, 1, :, :, None] * o_s + g[:, :, 2, :, :, None] * o_w
        return o.astype(q.dtype)

    o = lax.map(q_block, jnp.arange(S // Q_BLOCK))
    return o.transpose(1, 0, 2, 3, 4, 5).reshape(B, S, N_HEADS * HEAD_DIM)


def memory_xattn(h, mem_n, wq, wk, wv, wo):
    B, S = h.shape[:2]
    M = mem_n.shape[1]
    q = (h @ wq).reshape(B, S, N_XHEADS, XHEAD_DIM) * (XHEAD_DIM ** -0.5)
    k = (mem_n @ wk).reshape(B, M, N_XHEADS, XHEAD_DIM)
    v = (mem_n @ wv).reshape(B, M, N_XHEADS, XHEAD_DIM)
    s = jnp.einsum('bshd,bmhd->bhsm', q, k).astype(jnp.float32)
    p = jax.nn.softmax(s, axis=-1).astype(v.dtype)
    o = jnp.einsum('bhsm,bmhd->bshd', p, v).reshape(B, S, N_XHEADS * XHEAD_DIM)
    return o @ wo


def clamped_swiglu(u):
    x_glu = jnp.minimum(u[..., :D_FF], SWIGLU_LIMIT)
    x_lin = jnp.clip(u[..., D_FF:], -SWIGLU_LIMIT, SWIGLU_LIMIT)
    return x_glu * jax.nn.sigmoid(SWIGLU_ALPHA * x_glu) * (x_lin + 1.0)


def moe_ffn(h, w_router, b_router, w_up, b_up, w_down, b_down):
    T, D = h.shape
    logits = (h @ w_router + b_router).astype(jnp.float32)
    top_val, top_idx = lax.top_k(logits, TOP_K)
    gate = jax.nn.softmax(top_val, axis=-1)
    n_assign = T * TOP_K
    flat_e = top_idx.reshape(-1)
    flat_tok = jnp.repeat(jnp.arange(T, dtype=jnp.int32), TOP_K)
    flat_w = gate.reshape(-1).astype(h.dtype)
    order = jnp.argsort(flat_e)
    sorted_e = flat_e[order]
    counts = jnp.bincount(flat_e, length=N_EXPERTS)
    padded = (counts + EXPERT_BLOCK - 1) // EXPERT_BLOCK * EXPERT_BLOCK
    start = jnp.cumsum(counts) - counts
    pend = jnp.cumsum(padded)
    pstart = pend - padded
    dest = pstart[sorted_e] + jnp.arange(n_assign) - start[sorted_e]
    n_blk = -(-n_assign // EXPERT_BLOCK) + N_EXPERTS
    n_rows = n_blk * EXPERT_BLOCK
    row_tok = jnp.full((n_rows,), T, jnp.int32).at[dest].set(flat_tok[order])
    row_w = jnp.zeros((n_rows,), h.dtype).at[dest].set(flat_w[order])
    blk_e = jnp.minimum(jnp.searchsorted(pend, jnp.arange(n_blk) * EXPERT_BLOCK, side='right'), N_EXPERTS - 1)
    h_pad = jnp.concatenate([h, jnp.zeros((1, D), h.dtype)], axis=0)

    def expert_block(args):
        tok, w, e = args
        u = h_pad[tok] @ w_up[e] + b_up[e]
        y = clamped_swiglu(u) @ w_down[e] + b_down[e]
        return y * w[:, None]

    y = lax.map(expert_block, (row_tok.reshape(n_blk, EXPERT_BLOCK), row_w.reshape(n_blk, EXPERT_BLOCK), blk_e))
    return jax.ops.segment_sum(y.reshape(n_rows, D), row_tok, num_segments=T + 1)[:T]


def setup_inputs(seed: int = 0) -> dict:
    key = jax.random.key(seed)
    ks = jax.random.split(key, 32)

    def nrm(k, shape, scale):
        return jax.random.normal(k, shape, jnp.float32) * scale

    XW = N_XHEADS * XHEAD_DIM
    return {
        'x': nrm(ks[0], (BATCH, SEQ, D_MODEL), 1.0),
        'mem': nrm(ks[1], (BATCH, MEM_LEN, D_MODEL), 1.0),
        'g_mix': 1.0 + nrm(ks[2], (DEPTH, D_MODEL), 0.02),
        'w_in': nrm(ks[3], (DEPTH, D_MODEL, N_IN), D_MODEL ** -0.5),
        'conv_w': nrm(ks[4], (DEPTH, CONV_WIDTH, CONV_CH), CONV_WIDTH ** -0.5),
        'conv_b': nrm(ks[5], (DEPTH, CONV_CH), 0.02),
        'conv_ln_g': 1.0 + nrm(ks[6], (DEPTH, CONV_CH), 0.02),
        'conv_ln_b': nrm(ks[7], (DEPTH, CONV_CH), 0.02),
        'pe_cmp': nrm(ks[8], (DEPTH, 2, CMP_LEN, HEAD_DIM), 0.02),
        'w_cmp1': nrm(ks[9], (DEPTH, 2, CMP_LEN * HEAD_DIM, CMP_HIDDEN), (CMP_LEN * HEAD_DIM) ** -0.5),
        'b_cmp1': nrm(ks[10], (DEPTH, 2, CMP_HIDDEN), 0.02),
        'w_cmp2': nrm(ks[11], (DEPTH, 2, CMP_HIDDEN, HEAD_DIM), CMP_HIDDEN ** -0.5),
        'b_cmp2': nrm(ks[12], (DEPTH, 2, HEAD_DIM), 0.02),
        'w_out': nrm(ks[13], (DEPTH, MIX_WIDTH, D_MODEL), MIX_WIDTH ** -0.5),
        'g_x': 1.0 + nrm(ks[14], (DEPTH, D_MODEL), 0.02),
        'g_mem': 1.0 + nrm(ks[15], (DEPTH, D_MODEL), 0.02),
        'w_xq': nrm(ks[16], (DEPTH, D_MODEL, XW), D_MODEL ** -0.5),
        'w_xk': nrm(ks[17], (DEPTH, D_MODEL, XW), D_MODEL ** -0.5),
        'w_xv': nrm(ks[18], (DEPTH, D_MODEL, XW), D_MODEL ** -0.5),
        'w_xo': nrm(ks[19], (DEPTH, XW, D_MODEL), XW ** -0.5),
        'g_ffn': 1.0 + nrm(ks[20], (DEPTH, D_MODEL), 0.02),
        'w_router': nrm(ks[21], (DEPTH, D_MODEL, N_EXPERTS), D_MODEL ** -0.5),
        'b_router': nrm(ks[22], (DEPTH, N_EXPERTS), 0.01),
        'w_up': nrm(ks[23], (DEPTH, N_EXPERTS, D_MODEL, 2 * D_FF), D_MODEL ** -0.5),
        'b_up': nrm(ks[24], (DEPTH, N_EXPERTS, 2 * D_FF), 0.01),
        'w_down': nrm(ks[25], (DEPTH, N_EXPERTS, D_FF, D_MODEL), D_FF ** -0.5),
        'b_down': nrm(ks[26], (DEPTH, N_EXPERTS, D_MODEL), 0.01),
        'g_final': 1.0 + nrm(ks[27], (D_MODEL,), 0.02),
    }


def reference(x, mem, g_mix, w_in, conv_w, conv_b, conv_ln_g, conv_ln_b, pe_cmp, w_cmp1, b_cmp1,
              w_cmp2, b_cmp2, w_out, g_x, g_mem, w_xq, w_xk, w_xv, w_xo, g_ffn, w_router, b_router,
              w_up, b_up, w_down, b_down, g_final):
    B, S, D = x.shape
    split_at = [int(v) for v in np.cumsum([2 * CONV_CH, N_HEADS * HEAD_DIM] + [KV_WIDTH] * 6)]
    for l in range(DEPTH):
        h = rmsnorm(x, g_mix[l])
        u = h @ w_in[l]
        u_conv, u_q, k_c, v_c, k_s, v_s, k_w, v_w, u_g = jnp.split(u, split_at, axis=-1)
        kvs = [t.reshape(B, S, N_KV_HEADS, HEAD_DIM) for t in (k_c, v_c, k_s, v_s, k_w, v_w)]
        y_conv = conv_module(u_conv, conv_w[l], conv_b[l], conv_ln_g[l], conv_ln_b[l])
        y_nsa = nsa_mixer(u_q, kvs[0], kvs[1], kvs[2], kvs[3], kvs[4], kvs[5], u_g,
                          pe_cmp[l], w_cmp1[l], b_cmp1[l], w_cmp2[l], b_cmp2[l])
        x = x + jnp.concatenate([y_conv, y_nsa], axis=-1) @ w_out[l]
        x = x + memory_xattn(rmsnorm(x, g_x[l]), rmsnorm(mem, g_mem[l]), w_xq[l], w_xk[l], w_xv[l], w_xo[l])
        y = moe_ffn(rmsnorm(x, g_ffn[l]).reshape(B * S, D), w_router[l], b_router[l],
                    w_up[l], b_up[l], w_down[l], b_down[l])
        x = x + y.reshape(B, S, D)
    return rmsnorm(x, g_final)
```

```python
import functools

import numpy as np
import jax
import jax.numpy as jnp
from jax import lax
from jax.experimental import pallas as pl
from jax.experimental.pallas import tpu as pltpu

F32 = jnp.float32
BF16 = jnp.bfloat16

D_MODEL = 2048
CONV_CH = 1024
CONV_WIDTH = 31
N_HEADS = 16
HEAD_DIM = 64
N_KV_HEADS = 4
GQA_GROUP = 4
KV_WIDTH = N_KV_HEADS * HEAD_DIM
CMP_LEN = 32
CMP_STRIDE = 16
CMP_HIDDEN = 256
SEL_LEN = 64
SEL_TOPK = 16
WIN = 512
Q_BLOCK = 128
FORCE_SCORE = 1.0e4
N_XHEADS = 4
XHEAD_DIM = 128
XW = N_XHEADS * XHEAD_DIM
N_EXPERTS = 32
TOP_K = 4
D_FF = D_MODEL
SWIGLU_LIMIT = 7.0
SWIGLU_ALPHA = 1.702
NORM_EPS = 1e-5
NEG = -1e30
TAKEN = -3e38
LANES = 128
N_GATE = 3 * N_HEADS
COL_Q = 2 * CONV_CH
COL_KV = COL_Q + N_HEADS * HEAD_DIM
COL_GATE = COL_KV + 6 * KV_WIDTH
N_IN_PAD = COL_GATE + LANES
VMEM_LIMIT = 56 * 1024 * 1024


def _cparams(sem):
    return pltpu.CompilerParams(dimension_semantics=sem, vmem_limit_bytes=VMEM_LIMIT)


def _rms(x, g):
    return x * lax.rsqrt(jnp.mean(x * x, axis=-1, keepdims=True) + NORM_EPS) * g


def _sigmoid(x):
    return 1.0 / (1.0 + jnp.exp(-x))


def _dot(a, b):
    return jnp.dot(a, b, preferred_element_type=F32)


def _dot_nt(a, b):
    return lax.dot_general(a, b, (((1,), (1,)), ((), ())), preferred_element_type=F32)


def _split3(x):
    hi = x.astype(BF16)
    r1 = x - hi.astype(F32)
    mid = r1.astype(BF16)
    lo = (r1 - mid.astype(F32)).astype(BF16)
    return hi, mid, lo


def _inproj_body(x_ref, g_ref, w_ref, uc_ref, q_ref, kv_ref, gt_ref):
    h = _rms(x_ref[...], g_ref[...]).astype(BF16)
    uc_ref[...] = _dot(h, w_ref[:, 0:COL_Q])
    q_ref[...] = (_dot(h, w_ref[:, COL_Q:COL_KV]) * (HEAD_DIM ** -0.5)).astype(BF16)
    kv_ref[...] = _dot(h, w_ref[:, COL_KV:COL_GATE]).astype(BF16)
    gt_ref[...] = _dot(h, w_ref[:, COL_GATE:N_IN_PAD])


def _inproj(x2d, g, w_pad, tm=256):
    T = x2d.shape[0]
    return pl.pallas_call(
        _inproj_body,
        grid=(T // tm,),
        in_specs=[pl.BlockSpec((tm, D_MODEL), lambda i: (i, 0)),
                  pl.BlockSpec((1, D_MODEL), lambda i: (0, 0)),
                  pl.BlockSpec((D_MODEL, N_IN_PAD), lambda i: (0, 0))],
        out_specs=[pl.BlockSpec((tm, COL_Q), lambda i: (i, 0)),
                   pl.BlockSpec((tm, COL_KV - COL_Q), lambda i: (i, 0)),
                   pl.BlockSpec((tm, COL_GATE - COL_KV), lambda i: (i, 0)),
                   pl.BlockSpec((tm, LANES), lambda i: (i, 0))],
        out_shape=[jax.ShapeDtypeStruct((T, COL_Q), F32),
                   jax.ShapeDtypeStruct((T, COL_KV - COL_Q), BF16),
                   jax.ShapeDtypeStruct((T, COL_GATE - COL_KV), BF16),
                   jax.ShapeDtypeStruct((T, LANES), F32)],
        compiler_params=_cparams(("parallel",)),
        name="inproj",
    )(x2d, g, w_pad)


CONV_HALO = 32


def _conv_body(ua_ref, ug_ref, ha_ref, hg_ref, w_ref, b_ref, lg_ref, lb_ref, o_ref, buf_ref, *, ts):
    i = pl.program_id(1)
    halo = ha_ref[...] * _sigmoid(hg_ref[...])
    buf_ref[0:CONV_HALO, :] = jnp.where(i > 0, halo, 0.0)
    buf_ref[CONV_HALO:CONV_HALO + ts, :] = ua_ref[...] * _sigmoid(ug_ref[...])
    off = CONV_HALO - (CONV_WIDTH - 1)
    acc = jnp.zeros((ts, CONV_CH), F32) + b_ref[...]
    for j in range(CONV_WIDTH):
        acc = acc + w_ref[j:j + 1, :] * buf_ref[off + j:off + j + ts, :]
    mu = jnp.mean(acc, axis=-1, keepdims=True)
    d = acc - mu
    var = jnp.mean(d * d, axis=-1, keepdims=True)
    yn = d * lax.rsqrt(var + NORM_EPS) * lg_ref[...] + lb_ref[...]
    o_ref[...] = (yn * _sigmoid(yn)).astype(o_ref.dtype)


def _conv(uc, B, S, w, b, lg, lb, ts=256):
    nT = S // ts
    r = ts // CONV_HALO
    cur = lambda col: (lambda bb, i: (bb * nT + i, col))
    prev = lambda col: (lambda bb, i: (jnp.maximum((bb * nT + i) * r - 1, 0), col))
    vec = lambda n: pl.BlockSpec((n, CONV_CH), lambda bb, i: (0, 0))
    return pl.pallas_call(
        functools.partial(_conv_body, ts=ts),
        grid=(B, nT),
        in_specs=[pl.BlockSpec((ts, CONV_CH), cur(0)), pl.BlockSpec((ts, CONV_CH), cur(1)),
                  pl.BlockSpec((CONV_HALO, CONV_CH), prev(0)), pl.BlockSpec((CONV_HALO, CONV_CH), prev(1)),
                  vec(CONV_WIDTH), vec(1), vec(1), vec(1)],
        out_specs=pl.BlockSpec((ts, CONV_CH), lambda bb, i: (bb * nT + i, 0)),
        out_shape=jax.ShapeDtypeStruct((B * S, CONV_CH), BF16),
        scratch_shapes=[pltpu.VMEM((CONV_HALO + ts, CONV_CH), F32)],
        compiler_params=_cparams(("parallel", "parallel")),
        name="conv",
    )(uc, uc, uc, uc, w, b, lg, lb)


def _compress_body(x_ref, pe_ref, w1_ref, b1_ref, w2_ref, b2_ref, o_ref):
    half = (CMP_LEN // 2) * HEAD_DIM
    x = x_ref[...]
    first = _dot(x, w1_ref[0:half, :])
    second = _dot(x, w1_ref[half:2 * half, :])
    n = x.shape[0]
    second = pltpu.roll(second, n - 1, 0)
    pe8 = jnp.broadcast_to(pe_ref[...], (8, 2 * half)).astype(BF16)
    c1 = _dot(pe8, w1_ref[...])[0:1, :] + b1_ref[...]
    hid = first + second + c1
    act = (hid * _sigmoid(hid)).astype(BF16)
    o_ref[...] = (_dot(act, w2_ref[...]) + b2_ref[...]).astype(o_ref.dtype)


def _compress(kv16, pe, w1, b1, w2, b2):
    _, B, H, NC, W = kv16.shape
    sel = lambda *rest: (lambda a, bb, h: (a,) + rest)
    return pl.pallas_call(
        _compress_body,
        grid=(2, B, H),
        in_specs=[pl.BlockSpec((None, None, None, NC, W), lambda a, bb, h: (a, bb, h, 0, 0)),
                  pl.BlockSpec((None, 1, W * 2), sel(0, 0)),
                  pl.BlockSpec((None, W * 2, CMP_HIDDEN), sel(0, 0)),
                  pl.BlockSpec((None, 1, CMP_HIDDEN), sel(0, 0)),
                  pl.BlockSpec((None, CMP_HIDDEN, HEAD_DIM), sel(0, 0)),
                  pl.BlockSpec((None, 1, HEAD_DIM), sel(0, 0))],
        out_specs=pl.BlockSpec((None, None, None, NC, HEAD_DIM), lambda a, bb, h: (a, bb, h, 0, 0)),
        out_shape=jax.ShapeDtypeStruct((2, B, H, NC, HEAD_DIM), BF16),
        compiler_params=_cparams(("parallel", "parallel", "parallel")),
        name="compress",
    )(kv16, pe, w1, b1, w2, b2)


NSA_TK = 512


def _softmax_parts(s, mask):
    s = jnp.where(mask, s, NEG)
    m = jnp.max(s, axis=-1, keepdims=True)
    e = jnp.where(mask, jnp.exp(s - m), 0.0)
    return m, e, jnp.sum(e, axis=-1, keepdims=True)


def _nsa_body(q_ref, kc_ref, vc_ref, ks_ref, vs_ref, kw_ref, vw_ref, gate_ref, ovl_ref, exp_ref, slope_ref,
              o_ref, *, n_sel, n_cmp):
    i = pl.program_id(2)
    q0 = i * Q_BLOCK
    R = GQA_GROUP * Q_BLOCK
    q = q_ref[...].reshape(R, HEAD_DIM)
    slope = slope_ref[...]
    t = q0 + (lax.broadcasted_iota(jnp.int32, (R, 1), 0) & (Q_BLOCK - 1))

    nc = kc_ref.shape[0]
    c_ix = lax.broadcasted_iota(jnp.int32, (1, nc), 1)
    dist_c = t - (c_ix * CMP_STRIDE + (CMP_LEN - 1))
    mask_c = (dist_c >= 0) & (c_ix < n_cmp)
    s_c = _dot_nt(q, kc_ref[...]) - slope * dist_c.astype(F32)
    _, e_c, d_c = _softmax_parts(s_c, mask_c)
    p_c = e_c / jnp.where(d_c > 0, d_c, 1.0)
    o_c = _dot(p_c.astype(BF16), vc_ref[...])

    pc_sum = p_c[0:Q_BLOCK]
    for g in range(1, GQA_GROUP):
        pc_sum = pc_sum + p_c[g * Q_BLOCK:(g + 1) * Q_BLOCK]
    ovl = ovl_ref[...]
    imp = sum(_dot(part, ovl) for part in _split3(pc_sum))
    j_ix = lax.broadcasted_iota(jnp.int32, (1, n_sel), 1)
    j_f = j_ix.astype(F32)
    blk_t = (q0 + lax.broadcasted_iota(jnp.int32, (Q_BLOCK, 1), 0)) // SEL_LEN
    forced = (j_ix == 0) | (j_ix == blk_t) | (j_ix == blk_t - 1)
    score = jnp.where(forced, FORCE_SCORE, jnp.where(j_ix <= blk_t, imp, -FORCE_SCORE))
    sel = jnp.zeros((Q_BLOCK, n_sel), F32)
    for _ in range(min(SEL_TOPK, n_sel)):
        m = jnp.max(score, axis=-1, keepdims=True)
        first = jnp.min(jnp.where(score == m, j_f, float(n_sel)), axis=-1, keepdims=True)
        hit = j_f == first
        sel = jnp.where(hit & (m > -0.5 * FORCE_SCORE), 1.0, sel)
        score = jnp.where(hit, TAKEN, score)
    sel_b = sel.astype(BF16)

    def sweep(kt, carry):
        m_run, l_run, acc = carry
        k0 = pl.multiple_of(kt * NSA_TK, NSA_TK)
        k = ks_ref[pl.ds(k0, NSA_TK), :]
        v = vs_ref[pl.ds(k0, NSA_TK), :]
        dist = t - (k0 + lax.broadcasted_iota(jnp.int32, (1, NSA_TK), 1))
        chosen = _dot(sel_b, exp_ref[:, pl.ds(k0, NSA_TK)])
        chosen = jnp.concatenate([chosen] * GQA_GROUP, axis=0)
        mask = (chosen > 0.5) & (dist >= 0)
        s = jnp.where(mask, _dot_nt(q, k) - slope * dist.astype(F32), NEG)
        m_new = jnp.maximum(m_run, jnp.max(s, axis=-1, keepdims=True))
        alpha = jnp.exp(m_run - m_new)
        p = jnp.where(mask, jnp.exp(s - m_new), 0.0)
        l_new = alpha * l_run + jnp.sum(p, axis=-1, keepdims=True)
        acc = alpha * acc + _dot(p.astype(BF16), v)
        return m_new, l_new, acc

    n_kt = (q0 + Q_BLOCK - 1) // NSA_TK + 1
    init = (jnp.full((R, 1), NEG, F32), jnp.zeros((R, 1), F32), jnp.zeros((R, HEAD_DIM), F32))
    _, l_s, acc_s = lax.fori_loop(0, n_kt, sweep, init)
    o_s = acc_s / jnp.where(l_s > 0, l_s, 1.0)

    span = WIN + Q_BLOCK
    w0 = pl.multiple_of(jnp.maximum(q0 - WIN, 0), Q_BLOCK)
    dist_w = t - (w0 + lax.broadcasted_iota(jnp.int32, (1, span), 1))
    mask_w = (dist_w >= 0) & (dist_w < WIN)
    s_w = _dot_nt(q, kw_ref[pl.ds(w0, span), :]) - slope * dist_w.astype(F32)
    _, e_w, d_w = _softmax_parts(s_w, mask_w)
    o_w = _dot(e_w.astype(BF16), vw_ref[pl.ds(w0, span), :]) / jnp.where(d_w > 0, d_w, 1.0)

    gate = _sigmoid(gate_ref[...])
    for g in range(GQA_GROUP):
        rows = slice(g * Q_BLOCK, (g + 1) * Q_BLOCK)
        o = (gate[:, g:g + 1] * o_c[rows] + gate[:, GQA_GROUP + g:GQA_GROUP + g + 1] * o_s[rows]
             + gate[:, 2 * GQA_GROUP + g:2 * GQA_GROUP + g + 1] * o_w[rows])
        o_ref[g] = o.astype(o_ref.dtype)


def _selection_overlap(n_cmp_pad, n_cmp, n_sel):
    cs = np.arange(n_cmp_pad) * CMP_STRIDE
    ce = cs + CMP_LEN - 1
    js = np.arange(n_sel) * SEL_LEN
    m = (cs[:, None] <= js[None, :] + SEL_LEN - 1) & (ce[:, None] >= js[None, :])
    m &= (np.arange(n_cmp_pad) < n_cmp)[:, None]
    return m.astype(np.float32)


def _nsa(q5, cmp_kv, kv6, gates, B, S):
    n_sel = S // SEL_LEN
    n_cmp = (S - CMP_LEN) // CMP_STRIDE + 1
    nc = S // CMP_STRIDE
    nQ = S // Q_BLOCK
    ovl = jnp.asarray(_selection_overlap(nc, n_cmp, n_sel), BF16)
    expand = jnp.asarray((np.arange(S)[None, :] // SEL_LEN) == np.arange(n_sel)[:, None], BF16)
    sl = 2.0 ** (-8.0 * np.arange(1, N_HEADS + 1) / N_HEADS)
    slope = jnp.asarray(np.repeat(sl.reshape(N_KV_HEADS, GQA_GROUP, 1), Q_BLOCK, axis=1)
                        .reshape(N_KV_HEADS, GQA_GROUP * Q_BLOCK, 1), F32)
    cmp_spec = lambda a: pl.BlockSpec((None, None, None, nc, HEAD_DIM), lambda bb, h, i: (a, bb, h, 0, 0))
    kv_spec = lambda a: pl.BlockSpec((None, None, None, S, HEAD_DIM), lambda bb, h, i: (a, bb, h, 0, 0))
    return pl.pallas_call(
        functools.partial(_nsa_body, n_sel=n_sel, n_cmp=n_cmp),
        grid=(B, N_KV_HEADS, nQ),
        in_specs=[pl.BlockSpec((None, None, GQA_GROUP, Q_BLOCK, HEAD_DIM), lambda bb, h, i: (bb, h, 0, i, 0)),
                  cmp_spec(0), cmp_spec(1), kv_spec(2), kv_spec(3), kv_spec(4), kv_spec(5),
                  pl.BlockSpec((None, None, Q_BLOCK, 3 * GQA_GROUP), lambda bb, h, i: (bb, h, i, 0)),
                  pl.BlockSpec((nc, n_sel), lambda bb, h, i: (0, 0)),
                  pl.BlockSpec((n_sel, S), lambda bb, h, i: (0, 0)),
                  pl.BlockSpec((None, GQA_GROUP * Q_BLOCK, 1), lambda bb, h, i: (h, 0, 0))],
        out_specs=pl.BlockSpec((None, None, GQA_GROUP, Q_BLOCK, HEAD_DIM), lambda bb, h, i: (bb, h, 0, i, 0)),
        out_shape=jax.ShapeDtypeStruct((B, N_KV_HEADS, GQA_GROUP, S, HEAD_DIM), BF16),
        compiler_params=_cparams(("parallel", "parallel", "arbitrary")),
        name="nsa",
    )(q5, cmp_kv, cmp_kv, kv6, kv6, kv6, kv6, gates, ovl, expand, slope)


def _memkv_body(m_ref, g_ref, wk_ref, wv_ref, k_ref, v_ref):
    h = _rms(m_ref[...], g_ref[...]).astype(BF16)
    k_ref[...] = _dot(h, wk_ref[...]).astype(BF16)
    v_ref[...] = _dot(h, wv_ref[...]).astype(BF16)


def _memkv(mem2d, g, wk, wv, tm=256):
    R = mem2d.shape[0]
    wspec = pl.BlockSpec((D_MODEL, XW), lambda i: (0, 0))
    ospec = pl.BlockSpec((tm, XW), lambda i: (i, 0))
    return pl.pallas_call(
        _memkv_body,
        grid=(R // tm,),
        in_specs=[pl.BlockSpec((tm, D_MODEL), lambda i: (i, 0)), pl.BlockSpec((1, D_MODEL), lambda i: (0, 0)),
                  wspec, wspec],
        out_specs=[ospec, ospec],
        out_shape=[jax.ShapeDtypeStruct((R, XW), BF16)] * 2,
        compiler_params=_cparams(("parallel",)),
        name="memkv",
    )(mem2d, g, wk, wv)


def _mid_body(x_ref, yc_ref, yn_ref, wo_ref, gx_ref, wq_ref, mk_ref, mv_ref, wxo_ref, gf_ref, wr_ref, br_ref,
              x2_ref, h3_ref, idx_ref, gate_ref):
    x1 = x_ref[...] + _dot(yc_ref[...], wo_ref[0:CONV_CH, :]) + _dot(yn_ref[...], wo_ref[CONV_CH:2 * CONV_CH, :])
    h = _rms(x1, gx_ref[...]).astype(BF16)
    q = (_dot(h, wq_ref[...]) * (XHEAD_DIM ** -0.5)).astype(BF16)
    heads = []
    for hd in range(N_XHEADS):
        cols = slice(hd * XHEAD_DIM, (hd + 1) * XHEAD_DIM)
        s = _dot_nt(q[:, cols], mk_ref[:, cols])
        e = jnp.exp(s - jnp.max(s, axis=-1, keepdims=True))
        o = _dot(e.astype(BF16), mv_ref[:, cols]) / jnp.sum(e, axis=-1, keepdims=True)
        heads.append(o.astype(BF16))
    x2 = x1 + _dot(jnp.concatenate(heads, axis=-1), wxo_ref[...])
    x2_ref[...] = x2
    h3 = _rms(x2, gf_ref[...])
    h3_ref[...] = h3
    h_hi, h_mid, _ = _split3(h3)
    w_hi, w_mid, _ = _split3(wr_ref[...])
    logits = _dot(h_hi, w_hi) + _dot(h_hi, w_mid) + _dot(h_mid, w_hi) + br_ref[...]
    lane = lax.broadcasted_iota(jnp.int32, (1, LANES), 1)
    lane_f = lane.astype(F32)
    idx_out = jnp.zeros(logits.shape, F32)
    val_out = jnp.full(logits.shape, NEG, F32)
    for k in range(TOP_K):
        m = jnp.max(logits, axis=-1, keepdims=True)
        first = jnp.min(jnp.where(logits == m, lane_f, float(LANES)), axis=-1, keepdims=True)
        idx_out = jnp.where(lane == k, first, idx_out)
        val_out = jnp.where(lane == k, m, val_out)
        logits = jnp.where(lane_f == first, TAKEN, logits)
    e = jnp.where(lane < TOP_K, jnp.exp(val_out - jnp.max(val_out, axis=-1, keepdims=True)), 0.0)
    idx_ref[...] = idx_out.astype(jnp.int32)
    gate_ref[...] = e / jnp.sum(e, axis=-1, keepdims=True)


def _mid(x2d, yc, yn, w_out, g_x, w_xq, mk, mv, w_xo, g_ffn, w_r, b_r, B, S, tm=256):
    T = x2d.shape[0]
    nT = S // tm
    M = mk.shape[0] // B
    row = lambda w: pl.BlockSpec((tm, w), lambda i: (i, 0))
    full = lambda a: pl.BlockSpec(a.shape, lambda i: (0,) * a.ndim)
    memspec = pl.BlockSpec((M, XW), lambda i: (i // nT, 0))
    return pl.pallas_call(
        _mid_body,
        grid=(T // tm,),
        in_specs=[row(D_MODEL), row(CONV_CH), row(CONV_CH), full(w_out), full(g_x), full(w_xq), memspec, memspec,
                  full(w_xo), full(g_ffn), full(w_r), full(b_r)],
        out_specs=[row(D_MODEL), row(D_MODEL), row(LANES), row(LANES)],
        out_shape=[jax.ShapeDtypeStruct((T, D_MODEL), F32), jax.ShapeDtypeStruct((T, D_MODEL), F32),
                   jax.ShapeDtypeStruct((T, LANES), jnp.int32), jax.ShapeDtypeStruct((T, LANES), F32)],
        compiler_params=_cparams(("parallel",)),
        name="mid",
    )(x2d, yc, yn, w_out, g_x, w_xq, mk, mv, w_xo, g_ffn, w_r, b_r)


MOE_TM = 512
MOE_TF = 512


def _gmm_body(tile_e_ref, n_used_ref, row_tok_ref, h_hbm, roww_ref, wg_ref, wl_ref, bg_ref, bl_ref, wd_ref,
              bd_ref, y_ref, hbuf, hb16, acc_ref, sem):
    i = pl.program_id(0)
    f = pl.program_id(1)
    nf = pl.num_programs(1)

    @pl.when(i < n_used_ref[0])
    def _():
        @pl.when(f == 0)
        def _():
            def issue(r, c):
                tok = row_tok_ref[i * MOE_TM + r]
                pltpu.make_async_copy(h_hbm.at[pl.ds(tok, 1), :], hbuf.at[pl.ds(r, 1), :], sem).start()
                return c
            lax.fori_loop(0, MOE_TM, issue, 0)
            pltpu.make_async_copy(h_hbm.at[pl.ds(0, MOE_TM), :], hbuf, sem).wait()
            hb16[...] = hbuf[...].astype(BF16)
            acc_ref[...] = jnp.zeros_like(acc_ref)

        h = hb16[...]
        x_glu = jnp.minimum(_dot(h, wg_ref[...]) + bg_ref[...], SWIGLU_LIMIT)
        x_lin = jnp.clip(_dot(h, wl_ref[...]) + bl_ref[...], -SWIGLU_LIMIT, SWIGLU_LIMIT)
        act = x_glu * _sigmoid(SWIGLU_ALPHA * x_glu) * (x_lin + 1.0)
        acc_ref[...] += _dot(act.astype(BF16), wd_ref[...])

        @pl.when(f == nf - 1)
        def _():
            y_ref[...] = (acc_ref[...] + bd_ref[...]) * roww_ref[...]

    @pl.when((i >= n_used_ref[0]) & (f == 0))
    def _():
        y_ref[...] = jnp.zeros_like(y_ref)


def _gmm(tile_e, n_used, row_tok, h3, row_w, w_up, b_up, w_down, b_down):
    n_tiles = tile_e.shape[0]
    nf = D_FF // MOE_TF

    def live(i, f, te, nu):
        ok = i < nu[0]
        return jnp.where(ok, te[i], te[jnp.maximum(nu[0] - 1, 0)]), jnp.where(ok, f, nf - 1)

    def wg_map(i, f, te, nu, rt):
        e, ff = live(i, f, te, nu)
        return (e, 0, ff)

    def wl_map(i, f, te, nu, rt):
        e, ff = live(i, f, te, nu)
        return (e, 0, nf + ff)

    def wd_map(i, f, te, nu, rt):
        e, ff = live(i, f, te, nu)
        return (e, ff, 0)

    def bd_map(i, f, te, nu, rt):
        e, _ = live(i, f, te, nu)
        return (e, 0, 0)

    grid_spec = pltpu.PrefetchScalarGridSpec(
        num_scalar_prefetch=3,
        grid=(n_tiles, nf),
        in_specs=[pl.BlockSpec(memory_space=pl.ANY),
                  pl.BlockSpec((MOE_TM, 1), lambda i, f, te, nu, rt: (i, 0)),
                  pl.BlockSpec((None, D_MODEL, MOE_TF), wg_map),
                  pl.BlockSpec((None, D_MODEL, MOE_TF), wl_map),
                  pl.BlockSpec((None, 1, MOE_TF), wg_map),
                  pl.BlockSpec((None, 1, MOE_TF), wl_map),
                  pl.BlockSpec((None, MOE_TF, D_MODEL), wd_map),
                  pl.BlockSpec((None, 1, D_MODEL), bd_map)],
        out_specs=pl.BlockSpec((MOE_TM, D_MODEL), lambda i, f, te, nu, rt: (i, 0)),
        scratch_shapes=[pltpu.VMEM((MOE_TM, D_MODEL), F32), pltpu.VMEM((MOE_TM, D_MODEL), BF16),
                        pltpu.VMEM((MOE_TM, D_MODEL), F32), pltpu.SemaphoreType.DMA(())],
    )
    return pl.pallas_call(
        _gmm_body,
        grid_spec=grid_spec,
        out_shape=jax.ShapeDtypeStruct((n_tiles * MOE_TM, D_MODEL), F32),
        compiler_params=_cparams(("arbitrary", "arbitrary")),
        name="moe_gmm",
    )(tile_e, n_used, row_tok, h3, row_w, w_up, w_up, b_up, b_up, w_down, b_down)


CMB_TT = 128


def _combine_body(pos_ref, y_hbm, x_ref, g_ref, o_ref, buf, sem):
    i = pl.program_id(0)

    def issue(r, c):
        for k in range(TOP_K):
            p = pos_ref[(i * CMB_TT + r) * TOP_K + k]
            pltpu.make_async_copy(y_hbm.at[pl.ds(p, 1), :], buf.at[k, pl.ds(r, 1), :], sem).start()
        return c
    lax.fori_loop(0, CMB_TT, issue, 0)
    for k in range(TOP_K):
        pltpu.make_async_copy(y_hbm.at[pl.ds(0, CMB_TT), :], buf.at[k], sem).wait()
    y = buf[0]
    for k in range(1, TOP_K):
        y = y + buf[k]
    o_ref[...] = _rms(x_ref[...] + y, g_ref[...])


def _combine(pos, y_rows, x2, g_final):
    T = x2.shape[0]
    grid_spec = pltpu.PrefetchScalarGridSpec(
        num_scalar_prefetch=1,
        grid=(T // CMB_TT,),
        in_specs=[pl.BlockSpec(memory_space=pl.ANY),
                  pl.BlockSpec((CMB_TT, D_MODEL), lambda i, p: (i, 0)),
                  pl.BlockSpec((1, D_MODEL), lambda i, p: (0, 0))],
        out_specs=pl.BlockSpec((CMB_TT, D_MODEL), lambda i, p: (i, 0)),
        scratch_shapes=[pltpu.VMEM((TOP_K, CMB_TT, D_MODEL), F32), pltpu.SemaphoreType.DMA(())],
    )
    return pl.pallas_call(
        _combine_body,
        grid_spec=grid_spec,
        out_shape=jax.ShapeDtypeStruct((T, D_MODEL), F32),
        compiler_params=_cparams(("arbitrary",)),
        name="moe_combine",
    )(pos, y_rows, x2, g_final)


def _route(top_idx, gate):
    T = top_idx.shape[0]
    n_assign = T * TOP_K
    flat_e = top_idx.reshape(-1)
    order = jnp.argsort(flat_e, stable=True)
    sorted_e = flat_e[order]
    counts = jnp.bincount(flat_e, length=N_EXPERTS)
    padded = (counts + MOE_TM - 1) // MOE_TM * MOE_TM
    start = jnp.cumsum(counts) - counts
    pend = jnp.cumsum(padded)
    pstart = pend - padded
    dest = (pstart[sorted_e] + jnp.arange(n_assign) - start[sorted_e]).astype(jnp.int32)
    n_tiles = -(-n_assign // MOE_TM) + N_EXPERTS
    n_rows = n_tiles * MOE_TM
    flat_tok = jnp.repeat(jnp.arange(T, dtype=jnp.int32), TOP_K)
    row_tok = jnp.zeros((n_rows,), jnp.int32).at[dest].set(flat_tok[order])
    row_w = jnp.zeros((n_rows,), F32).at[dest].set(gate.reshape(-1)[order])
    tile_e = jnp.minimum(jnp.searchsorted(pend, jnp.arange(n_tiles) * MOE_TM, side='right'),
                         N_EXPERTS - 1).astype(jnp.int32)
    n_used = (pend[-1] // MOE_TM).astype(jnp.int32).reshape(1)
    pos = jnp.zeros((n_assign,), jnp.int32).at[order].set(dest)
    return tile_e, n_used, row_tok, row_w.reshape(n_rows, 1), pos


def _layer(x, mem, g_mix, w_in, conv_w, conv_b, conv_ln_g, conv_ln_b, pe_cmp, w_cmp1, b_cmp1, w_cmp2, b_cmp2,
           w_out, g_x, g_mem, w_xq, w_xk, w_xv, w_xo, g_ffn, w_router, b_router, w_up, b_up, w_down, b_down):
    B, S, D = x.shape
    T = B * S
    x2d = x.reshape(T, D)
    row = lambda v: v.reshape(1, -1)

    w_pad = jnp.pad(w_in, ((0, 0), (0, N_IN_PAD - w_in.shape[1]))).astype(BF16)
    uc, q, kv, gt = _inproj(x2d, row(g_mix), w_pad)

    y_conv = _conv(uc, B, S, conv_w, row(conv_b), row(conv_ln_g), row(conv_ln_b))

    q5 = q.reshape(B, S, N_KV_HEADS, GQA_GROUP, HEAD_DIM).transpose(0, 2, 3, 1, 4)
    kv6 = kv.reshape(B, S, 6, N_KV_HEADS, HEAD_DIM).transpose(2, 0, 3, 1, 4)
    kv16 = kv6.reshape(6, B, N_KV_HEADS, S // CMP_STRIDE, CMP_STRIDE * HEAD_DIM)
    cmp_kv = _compress(kv16, pe_cmp.reshape(2, 1, CMP_LEN * HEAD_DIM), w_cmp1.astype(BF16),
                       b_cmp1.reshape(2, 1, CMP_HIDDEN), w_cmp2.astype(BF16), b_cmp2.reshape(2, 1, HEAD_DIM))
    gates = gt[:, :N_GATE].reshape(B, S, 3, N_KV_HEADS, GQA_GROUP).transpose(0, 3, 1, 2, 4) \
        .reshape(B, N_KV_HEADS, S, 3 * GQA_GROUP)
    o5 = _nsa(q5, cmp_kv, kv6, gates, B, S)
    y_nsa = o5.transpose(0, 3, 1, 2, 4).reshape(T, N_HEADS * HEAD_DIM)

    M = mem.shape[1]
    mk, mv = _memkv(mem.reshape(B * M, D), row(g_mem), w_xk.astype(BF16), w_xv.astype(BF16))
    w_r = jnp.pad(w_router, ((0, 0), (0, LANES - N_EXPERTS)))
    b_r = jnp.pad(b_router, (0, LANES - N_EXPERTS), constant_values=NEG).reshape(1, LANES)
    x2, h3, idx, gate = _mid(x2d, y_conv, y_nsa, w_out.astype(BF16), row(g_x), w_xq.astype(BF16), mk, mv,
                             w_xo.astype(BF16), row(g_ffn), w_r, b_r, B, S)

    tile_e, n_used, row_tok, row_w, pos = _route(idx[:, :TOP_K], gate[:, :TOP_K])
    y_rows = _gmm(tile_e, n_used, row_tok, h3, row_w, w_up.astype(BF16), b_up.reshape(N_EXPERTS, 1, 2 * D_FF),
                  w_down.astype(BF16), b_down.reshape(N_EXPERTS, 1, D_MODEL))
    return x2, y_rows, pos


def kernel(x, mem, g_mix, w_in, conv_w, conv_b, conv_ln_g, conv_ln_b, pe_cmp, w_cmp1, b_cmp1, w_cmp2, b_cmp2, w_out, g_x, g_mem, w_xq, w_xk, w_xv, w_xo, g_ffn, w_router, b_router, w_up, b_up, w_down, b_down, g_final):
    B, S, D = x.shape
    assert g_mix.shape[0] == 1, "single-layer block"
    x2, y_rows, pos = _layer(x, mem, g_mix[0], w_in[0], conv_w[0], conv_b[0], conv_ln_g[0], conv_ln_b[0],
                             pe_cmp[0], w_cmp1[0], b_cmp1[0], w_cmp2[0], b_cmp2[0], w_out[0], g_x[0], g_mem[0],
                             w_xq[0], w_xk[0], w_xv[0], w_xo[0], g_ffn[0], w_router[0], b_router[0],
                             w_up[0], b_up[0], w_down[0], b_down[0])
    out = _combine(pos, y_rows, x2, g_final.reshape(1, D))
    return out.reshape(B, S, D)
```

```python
import functools

import numpy as np
import jax
import jax.numpy as jnp
from jax import lax
from jax.experimental import pallas as pl
from jax.experimental.pallas import tpu as pltpu

F32 = jnp.float32
BF16 = jnp.bfloat16

D_MODEL = 2048
CONV_CH = 1024
CONV_WIDTH = 31
N_HEADS = 16
HEAD_DIM = 64
N_KV_HEADS = 4
GQA_GROUP = 4
KV_WIDTH = N_KV_HEADS * HEAD_DIM
CMP_LEN = 32
CMP_STRIDE = 16
CMP_HIDDEN = 256
SEL_LEN = 64
SEL_TOPK = 16
N_FORCED = 3
WIN = 512
Q_BLOCK = 128
FORCE_SCORE = 1.0e4
N_XHEADS = 4
XHEAD_DIM = 128
XW = N_XHEADS * XHEAD_DIM
N_EXPERTS = 32
TOP_K = 4
D_FF = D_MODEL
SWIGLU_LIMIT = 7.0
SWIGLU_ALPHA = 1.702
NORM_EPS = 1e-5
NEG = -1e30
TAKEN = -3e38
LANES = 128
N_GATE = 3 * N_HEADS
COL_Q = 2 * CONV_CH
COL_KV = COL_Q + N_HEADS * HEAD_DIM
COL_GATE = COL_KV + 6 * KV_WIDTH
N_IN_PAD = COL_GATE + LANES
VMEM_LIMIT = 56 * 1024 * 1024


def _cparams(sem):
    return pltpu.CompilerParams(dimension_semantics=sem, vmem_limit_bytes=VMEM_LIMIT)


def _rms(x, g):
    return x * lax.rsqrt(jnp.mean(x * x, axis=-1, keepdims=True) + NORM_EPS) * g


def _sigmoid(x):
    return 1.0 / (1.0 + jnp.exp(-x))


def _dot(a, b):
    return jnp.dot(a, b, preferred_element_type=F32)


def _dot_nt(a, b):
    return lax.dot_general(a, b, (((1,), (1,)), ((), ())), preferred_element_type=F32)


def _split3(x):
    hi = x.astype(BF16)
    r1 = x - hi.astype(F32)
    mid = r1.astype(BF16)
    lo = (r1 - mid.astype(F32)).astype(BF16)
    return hi, mid, lo


def _inproj_body(x_ref, g_ref, w_ref, uc_ref, q_ref, kv_ref, gt_ref):
    h = _rms(x_ref[...], g_ref[...]).astype(BF16)
    uc_ref[...] = _dot(h, w_ref[:, 0:COL_Q])
    q_ref[...] = (_dot(h, w_ref[:, COL_Q:COL_KV]) * (HEAD_DIM ** -0.5)).astype(BF16)
    kv_ref[...] = _dot(h, w_ref[:, COL_KV:COL_GATE]).astype(BF16)
    gt_ref[...] = _dot(h, w_ref[:, COL_GATE:N_IN_PAD])


def _inproj(x2d, g, w_pad, tm=256):
    T = x2d.shape[0]
    return pl.pallas_call(
        _inproj_body,
        grid=(T // tm,),
        in_specs=[pl.BlockSpec((tm, D_MODEL), lambda i: (i, 0)),
                  pl.BlockSpec((1, D_MODEL), lambda i: (0, 0)),
                  pl.BlockSpec((D_MODEL, N_IN_PAD), lambda i: (0, 0))],
        out_specs=[pl.BlockSpec((tm, COL_Q), lambda i: (i, 0)),
                   pl.BlockSpec((tm, COL_KV - COL_Q), lambda i: (i, 0)),
                   pl.BlockSpec((tm, COL_GATE - COL_KV), lambda i: (i, 0)),
                   pl.BlockSpec((tm, LANES), lambda i: (i, 0))],
        out_shape=[jax.ShapeDtypeStruct((T, COL_Q), F32),
                   jax.ShapeDtypeStruct((T, COL_KV - COL_Q), BF16),
                   jax.ShapeDtypeStruct((T, COL_GATE - COL_KV), BF16),
                   jax.ShapeDtypeStruct((T, LANES), F32)],
        compiler_params=_cparams(("parallel",)),
        name="inproj",
    )(x2d, g, w_pad)


CONV_HALO = 32


def _conv_body(ua_ref, ug_ref, ha_ref, hg_ref, w_ref, b_ref, lg_ref, lb_ref, o_ref, buf_ref, *, ts):
    i = pl.program_id(1)
    halo = ha_ref[...] * _sigmoid(hg_ref[...])
    buf_ref[0:CONV_HALO, :] = jnp.where(i > 0, halo, 0.0)
    buf_ref[CONV_HALO:CONV_HALO + ts, :] = ua_ref[...] * _sigmoid(ug_ref[...])
    off = CONV_HALO - (CONV_WIDTH - 1)
    acc = jnp.zeros((ts, CONV_CH), F32) + b_ref[...]
    for j in range(CONV_WIDTH):
        acc = acc + w_ref[j:j + 1, :] * buf_ref[off + j:off + j + ts, :]
    mu = jnp.mean(acc, axis=-1, keepdims=True)
    d = acc - mu
    var = jnp.mean(d * d, axis=-1, keepdims=True)
    yn = d * lax.rsqrt(var + NORM_EPS) * lg_ref[...] + lb_ref[...]
    o_ref[...] = (yn * _sigmoid(yn)).astype(o_ref.dtype)


def _conv(uc, B, S, w, b, lg, lb, ts=256):
    nT = S // ts
    r = ts // CONV_HALO
    cur = lambda col: (lambda bb, i: (bb * nT + i, col))
    prev = lambda col: (lambda bb, i: (jnp.maximum((bb * nT + i) * r - 1, 0), col))
    vec = lambda n: pl.BlockSpec((n, CONV_CH), lambda bb, i: (0, 0))
    return pl.pallas_call(
        functools.partial(_conv_body, ts=ts),
        grid=(B, nT),
        in_specs=[pl.BlockSpec((ts, CONV_CH), cur(0)), pl.BlockSpec((ts, CONV_CH), cur(1)),
                  pl.BlockSpec((CONV_HALO, CONV_CH), prev(0)), pl.BlockSpec((CONV_HALO, CONV_CH), prev(1)),
                  vec(CONV_WIDTH), vec(1), vec(1), vec(1)],
        out_specs=pl.BlockSpec((ts, CONV_CH), lambda bb, i: (bb * nT + i, 0)),
        out_shape=jax.ShapeDtypeStruct((B * S, CONV_CH), BF16),
        scratch_shapes=[pltpu.VMEM((CONV_HALO + ts, CONV_CH), F32)],
        compiler_params=_cparams(("parallel", "parallel")),
        name="conv",
    )(uc, uc, uc, uc, w, b, lg, lb)


N_BIAS = 6


def _pos_columns(pos_hi, pos_lo):
    n = pos_hi.shape[0]
    cols = np.zeros((n, LANES), np.float32)
    for half in (0, HEAD_DIM):
        for k in range(N_BIAS):
            cols[:, half + k] = pos_hi if k % 2 == 0 else pos_lo
    return cols


def _slope_columns():
    sl = jnp.asarray(2.0 ** (-8.0 * np.arange(1, N_HEADS + 1) / N_HEADS), F32)
    parts = _split3(sl)
    cols = jnp.zeros((N_HEADS, LANES), BF16)
    for half in (0, HEAD_DIM):
        for k in range(N_BIAS):
            cols = cols.at[:, half + k].set(parts[k // 2])
    cols = cols.reshape(N_KV_HEADS, GQA_GROUP, 1, LANES)
    return jnp.broadcast_to(cols, (N_KV_HEADS, GQA_GROUP, Q_BLOCK, LANES)).reshape(N_KV_HEADS, GQA_GROUP * Q_BLOCK, LANES)


def _compress_body(x_ref, pe_ref, w1_ref, b1_ref, w2_ref, b2_ref, c_ref, o_ref):
    half = (CMP_LEN // 2) * HEAD_DIM
    x = x_ref[...]
    first = _dot(x, w1_ref[0:half, :])
    second = _dot(x, w1_ref[half:2 * half, :])
    n = x.shape[0]
    second = pltpu.roll(second, n - 1, 0)
    pe8 = jnp.broadcast_to(pe_ref[...], (8, 2 * half)).astype(BF16)
    c1 = _dot(pe8, w1_ref[...])[0:1, :] + b1_ref[...]
    hid = first + second + c1
    act = (hid * _sigmoid(hid)).astype(BF16)
    o_ref[...] = (_dot(act, w2_ref[...]) + b2_ref[...] + c_ref[...]).astype(o_ref.dtype)


def _compress(kv16, pe, w1, b1, w2, b2):
    _, B, H, NC, W = kv16.shape
    c = np.arange(NC)
    consts = np.zeros((2, NC, LANES), np.float32)
    consts[0] = _pos_columns((c // 16) * 256.0, (c % 16) * 16.0)
    consts[0, :, :HEAD_DIM] = 0.0
    consts[1, :, HEAD_DIM:] = 1.0
    w2p = jnp.pad(w2, ((0, 0), (0, 0), (0, LANES - HEAD_DIM)))
    b2p = jnp.pad(b2, ((0, 0), (0, 0), (0, LANES - HEAD_DIM)))
    sel = lambda *rest: (lambda a, bb, h: (a,) + rest)
    return pl.pallas_call(
        _compress_body,
        grid=(2, B, H),
        in_specs=[pl.BlockSpec((None, None, None, NC, W), lambda a, bb, h: (a, bb, h, 0, 0)),
                  pl.BlockSpec((None, 1, W * 2), sel(0, 0)),
                  pl.BlockSpec((None, W * 2, CMP_HIDDEN), sel(0, 0)),
                  pl.BlockSpec((None, 1, CMP_HIDDEN), sel(0, 0)),
                  pl.BlockSpec((None, CMP_HIDDEN, LANES), sel(0, 0)),
                  pl.BlockSpec((None, 1, LANES), sel(0, 0)),
                  pl.BlockSpec((None, NC, LANES), sel(0, 0))],
        out_specs=pl.BlockSpec((None, None, None, NC, LANES), lambda a, bb, h: (a, bb, h, 0, 0)),
        out_shape=jax.ShapeDtypeStruct((2, B, H, NC, LANES), BF16),
        compiler_params=_cparams(("parallel", "parallel", "parallel")),
        name="compress",
    )(kv16, pe, w1, b1, w2p, b2p, jnp.asarray(consts))


NSA_TK = 512


def _nsa_body(q_ref, kc_ref, vc_ref, ks_ref, vs_ref, kw_ref, vw_ref, gate_ref, ovl_ref, hot_ref, pos_ref,
              slope_ref, o_ref, *, n_sel, n_cmp):
    par = pl.program_id(1) % 2
    i = pl.program_id(2)
    q0 = i * Q_BLOCK
    R = GQA_GROUP * Q_BLOCK
    lane = lax.broadcasted_iota(jnp.int32, (1, LANES), 1)
    low = lane < HEAD_DIM
    mine = (lane // HEAD_DIM) == par
    t = q0 + (lax.broadcasted_iota(jnp.int32, (R, 1), 0) & (Q_BLOCK - 1))

    qb = q_ref[...]
    in_low, in_mine = [], []
    for g in range(GQA_GROUP):
        col = qb[:, LANES * (g // 2):LANES * (g // 2 + 1)]
        swapped = pltpu.roll(col, HEAD_DIM, 1)
        in_low.append(col if g % 2 == 0 else swapped)
        in_mine.append(jnp.where((g % 2) == par, col, swapped))
    slope_cols = slope_ref[...]
    q_low = jnp.where(low, jnp.concatenate(in_low, axis=0), slope_cols)
    q_par = jnp.where(mine, jnp.concatenate(in_mine, axis=0), slope_cols)

    nc = kc_ref.shape[0]
    c_ix = lax.broadcasted_iota(jnp.int32, (1, nc), 1)
    mask_c = (t >= c_ix * CMP_STRIDE + (CMP_LEN - 1)) & (c_ix < n_cmp)
    s_c = jnp.where(mask_c, _dot_nt(q_low, kc_ref[...]), NEG)
    e_c = jnp.where(mask_c, jnp.exp(s_c - jnp.max(s_c, axis=-1, keepdims=True)), 0.0)
    d_c = jnp.sum(e_c, axis=-1, keepdims=True)
    p_c = e_c / jnp.where(d_c > 0, d_c, 1.0)
    o_c = _dot(p_c.astype(BF16), vc_ref[...])

    pc_sum = p_c[0:Q_BLOCK]
    for g in range(1, GQA_GROUP):
        pc_sum = pc_sum + p_c[g * Q_BLOCK:(g + 1) * Q_BLOCK]
    ovl = ovl_ref[...]
    imp = sum(_dot_nt(ovl, part) for part in _split3(pc_sum))
    j_ix = lax.broadcasted_iota(jnp.int32, (n_sel, 1), 0)
    j_f = j_ix.astype(F32)
    blk_t = (q0 + lax.broadcasted_iota(jnp.int32, (1, Q_BLOCK), 1)) // SEL_LEN
    forced = (j_ix == 0) | (j_ix == blk_t) | (j_ix == blk_t - 1)
    score = jnp.where((j_ix <= blk_t) & ~forced, imp, -FORCE_SCORE)
    sel = forced
    for _ in range(min(SEL_TOPK, n_sel) - N_FORCED):
        m = jnp.max(score, axis=0, keepdims=True)
        first = jnp.min(jnp.where(score == m, j_f, float(n_sel)), axis=0, keepdims=True)
        hit = j_f == first
        sel = sel | (hit & (m > -0.5 * FORCE_SCORE))
        score = jnp.where(hit, TAKEN, score)
    sel_bias = jnp.where(sel, 0.0, NEG).T.astype(BF16)
    q_sel = jnp.concatenate([q_par, jnp.concatenate([sel_bias] * GQA_GROUP, axis=0)], axis=1)

    def k_tile(k_ref, start, size):
        return jnp.where(mine, k_ref[pl.ds(start, size), :], pos_ref[pl.ds(start, size), :])

    def v_tile(v_ref, start, size):
        return jnp.where(mine, v_ref[pl.ds(start, size), :], jnp.ones((), BF16))

    span = WIN + Q_BLOCK
    w0 = pl.multiple_of(jnp.maximum(q0 - WIN, 0), Q_BLOCK)
    dist_w = t - (w0 + lax.broadcasted_iota(jnp.int32, (1, span), 1))
    mask_w = (dist_w >= 0) & (dist_w < WIN)
    s_w = jnp.where(mask_w, _dot_nt(q_par, k_tile(kw_ref, w0, span)), NEG)
    e_w = jnp.exp(s_w - jnp.max(s_w, axis=-1, keepdims=True))
    acc_w = _dot(e_w.astype(BF16), v_tile(vw_ref, w0, span))

    def sweep_tile(k0, carry, causal):
        m_run, acc = carry
        k_aug = jnp.concatenate([k_tile(ks_ref, k0, NSA_TK), hot_ref[pl.ds(k0, NSA_TK), :]], axis=1)
        s = _dot_nt(q_sel, k_aug)
        if causal:
            s = jnp.where(t >= k0 + lax.broadcasted_iota(jnp.int32, (1, NSA_TK), 1), s, NEG)
        m_new = jnp.maximum(m_run, jnp.max(s, axis=-1, keepdims=True))
        p = jnp.exp(s - m_new)
        acc = jnp.exp(m_run - m_new) * acc + _dot(p.astype(BF16), v_tile(vs_ref, k0, NSA_TK))
        return m_new, acc

    n_full = q0 // NSA_TK
    carry = (jnp.full((R, 1), NEG, F32), jnp.zeros((R, LANES), F32))
    carry = lax.fori_loop(0, n_full, lambda kt, c: sweep_tile(pl.multiple_of(kt * NSA_TK, NSA_TK), c, False), carry)
    _, acc_s = sweep_tile(pl.multiple_of(n_full * NSA_TK, NSA_TK), carry, True)

    def normalised(acc):
        return acc / jnp.where(mine, pltpu.roll(acc, HEAD_DIM, 1), 1.0)

    o_sw_all = (normalised(acc_s), normalised(acc_w))
    gate = _sigmoid(gate_ref[...])
    placed = []
    for g in range(GQA_GROUP):
        rows = slice(g * Q_BLOCK, (g + 1) * Q_BLOCK)
        o_sw = (gate[:, GQA_GROUP + g:GQA_GROUP + g + 1] * o_sw_all[0][rows]
                + gate[:, 2 * GQA_GROUP + g:2 * GQA_GROUP + g + 1] * o_sw_all[1][rows])
        o_cg = gate[:, g:g + 1] * o_c[rows]
        if g % 2 == 1:
            o_cg = pltpu.roll(o_cg, HEAD_DIM, 1)
        o_sw = jnp.where((g % 2) == par, o_sw, pltpu.roll(o_sw, HEAD_DIM, 1))
        placed.append(o_cg + o_sw)
    out = [jnp.where(low, placed[2 * c], placed[2 * c + 1]) for c in range(GQA_GROUP // 2)]
    o_ref[...] = jnp.concatenate(out, axis=1).astype(o_ref.dtype)


def _selection_overlap_t(n_cmp_pad, n_cmp, n_sel):
    cs = np.arange(n_cmp_pad) * CMP_STRIDE
    ce = cs + CMP_LEN - 1
    js = np.arange(n_sel) * SEL_LEN
    m = (cs[None, :] <= js[:, None] + SEL_LEN - 1) & (ce[None, :] >= js[:, None])
    m &= (np.arange(n_cmp_pad) < n_cmp)[None, :]
    return m.astype(np.float32)


def _nsa(q, cmp_kv, kv, gates, B, S):
    n_sel = S // SEL_LEN
    assert n_sel >= SEL_TOPK and S % NSA_TK == 0 and S >= WIN + Q_BLOCK
    n_cmp = (S - CMP_LEN) // CMP_STRIDE + 1
    nc = S // CMP_STRIDE
    nQ = S // Q_BLOCK
    T = B * S
    ovl = jnp.asarray(_selection_overlap_t(nc, n_cmp, n_sel), BF16)
    pos = np.arange(S)
    hot = jnp.asarray((pos[:, None] // SEL_LEN) == np.arange(n_sel)[None, :], BF16)
    pos_cols = jnp.asarray(_pos_columns((pos // 64) * 64.0, (pos % 64) * 1.0), BF16)
    cmp_spec = lambda a: pl.BlockSpec((None, None, None, nc, LANES), lambda bb, h, i: (a, bb, h, 0, 0))
    kv_spec = lambda a: pl.BlockSpec((S, LANES), lambda bb, h, i: (bb, 2 * a + h // 2))
    const = lambda shape: pl.BlockSpec(shape, lambda bb, h, i: (0,) * len(shape))
    return pl.pallas_call(
        functools.partial(_nsa_body, n_sel=n_sel, n_cmp=n_cmp),
        grid=(B, N_KV_HEADS, nQ),
        in_specs=[pl.BlockSpec((Q_BLOCK, GQA_GROUP * HEAD_DIM), lambda bb, h, i: (bb * nQ + i, h)),
                  cmp_spec(0), cmp_spec(1), kv_spec(2), kv_spec(3), kv_spec(4), kv_spec(5),
                  pl.BlockSpec((None, None, Q_BLOCK, 3 * GQA_GROUP), lambda bb, h, i: (bb, h, i, 0)),
                  const((n_sel, nc)), const((S, n_sel)), const((S, LANES)),
                  pl.BlockSpec((None, GQA_GROUP * Q_BLOCK, LANES), lambda bb, h, i: (h, 0, 0))],
        out_specs=pl.BlockSpec((Q_BLOCK, GQA_GROUP * HEAD_DIM), lambda bb, h, i: (bb * nQ + i, h)),
        out_shape=jax.ShapeDtypeStruct((T, N_HEADS * HEAD_DIM), BF16),
        compiler_params=_cparams(("parallel", "parallel", "arbitrary")),
        name="nsa",
    )(q, cmp_kv, cmp_kv, kv, kv, kv, kv, gates, ovl, hot, pos_cols, _slope_columns())


def _memkv_body(m_ref, g_ref, wk_ref, wv_ref, k_ref, v_ref):
    h = _rms(m_ref[...], g_ref[...]).astype(BF16)
    k_ref[...] = _dot(h, wk_ref[...]).astype(BF16)
    v_ref[...] = _dot(h, wv_ref[...]).astype(BF16)


def _memkv(mem2d, g, wk, wv, tm=256):
    R = mem2d.shape[0]
    wspec = pl.BlockSpec((D_MODEL, XW), lambda i: (0, 0))
    ospec = pl.BlockSpec((tm, XW), lambda i: (i, 0))
    return pl.pallas_call(
        _memkv_body,
        grid=(R // tm,),
        in_specs=[pl.BlockSpec((tm, D_MODEL), lambda i: (i, 0)), pl.BlockSpec((1, D_MODEL), lambda i: (0, 0)),
                  wspec, wspec],
        out_specs=[ospec, ospec],
        out_shape=[jax.ShapeDtypeStruct((R, XW), BF16)] * 2,
        compiler_params=_cparams(("parallel",)),
        name="memkv",
    )(mem2d, g, wk, wv)


def _mid_body(x_ref, yc_ref, yn_ref, wo_ref, gx_ref, wq_ref, mk_ref, mv_ref, wxo_ref, gf_ref, wr_ref, br_ref,
              x2_ref, h3_ref, idx_ref, gate_ref):
    x1 = x_ref[...] + _dot(yc_ref[...], wo_ref[0:CONV_CH, :]) + _dot(yn_ref[...], wo_ref[CONV_CH:2 * CONV_CH, :])
    h = _rms(x1, gx_ref[...]).astype(BF16)
    q = (_dot(h, wq_ref[...]) * (XHEAD_DIM ** -0.5)).astype(BF16)
    heads = []
    for hd in range(N_XHEADS):
        cols = slice(hd * XHEAD_DIM, (hd + 1) * XHEAD_DIM)
        s = _dot_nt(q[:, cols], mk_ref[:, cols])
        e = jnp.exp(s - jnp.max(s, axis=-1, keepdims=True))
        o = _dot(e.astype(BF16), mv_ref[:, cols]) / jnp.sum(e, axis=-1, keepdims=True)
        heads.append(o.astype(BF16))
    x2 = x1 + _dot(jnp.concatenate(heads, axis=-1), wxo_ref[...])
    x2_ref[...] = x2
    h3 = _rms(x2, gf_ref[...])
    h3_ref[...] = h3
    h_hi, h_mid, _ = _split3(h3)
    w_hi, w_mid, _ = _split3(wr_ref[...])
    logits = _dot(h_hi, w_hi) + _dot(h_hi, w_mid) + _dot(h_mid, w_hi) + br_ref[...]
    lane = lax.broadcasted_iota(jnp.int32, (1, LANES), 1)
    lane_f = lane.astype(F32)
    idx_out = jnp.zeros(logits.shape, F32)
    val_out = jnp.full(logits.shape, NEG, F32)
    for k in range(TOP_K):
        m = jnp.max(logits, axis=-1, keepdims=True)
        first = jnp.min(jnp.where(logits == m, lane_f, float(LANES)), axis=-1, keepdims=True)
        idx_out = jnp.where(lane == k, first, idx_out)
        val_out = jnp.where(lane == k, m, val_out)
        logits = jnp.where(lane_f == first, TAKEN, logits)
    e = jnp.where(lane < TOP_K, jnp.exp(val_out - jnp.max(val_out, axis=-1, keepdims=True)), 0.0)
    idx_ref[...] = idx_out.astype(jnp.int32)
    gate_ref[...] = e / jnp.sum(e, axis=-1, keepdims=True)


def _mid(x2d, yc, yn, w_out, g_x, w_xq, mk, mv, w_xo, g_ffn, w_r, b_r, B, S, tm=256):
    T = x2d.shape[0]
    nT = S // tm
    M = mk.shape[0] // B
    row = lambda w: pl.BlockSpec((tm, w), lambda i: (i, 0))
    full = lambda a: pl.BlockSpec(a.shape, lambda i: (0,) * a.ndim)
    memspec = pl.BlockSpec((M, XW), lambda i: (i // nT, 0))
    return pl.pallas_call(
        _mid_body,
        grid=(T // tm,),
        in_specs=[row(D_MODEL), row(CONV_CH), row(CONV_CH), full(w_out), full(g_x), full(w_xq), memspec, memspec,
                  full(w_xo), full(g_ffn), full(w_r), full(b_r)],
        out_specs=[row(D_MODEL), row(D_MODEL), row(LANES), row(LANES)],
        out_shape=[jax.ShapeDtypeStruct((T, D_MODEL), F32), jax.ShapeDtypeStruct((T, D_MODEL), F32),
                   jax.ShapeDtypeStruct((T, LANES), jnp.int32), jax.ShapeDtypeStruct((T, LANES), F32)],
        compiler_params=_cparams(("parallel",)),
        name="mid",
    )(x2d, yc, yn, w_out, g_x, w_xq, mk, mv, w_xo, g_ffn, w_r, b_r)


MOE_TM = 512
MOE_TF = 512


def _gmm_body(tile_e_ref, tile_base_ref, n_used_ref, tok_ref, h_hbm, wg_ref, wl_ref, bg_ref, bl_ref, wd_ref,
              bd_ref, y_ref, hbuf, hb16, acc_ref, sem):
    i = pl.program_id(0)
    f = pl.program_id(1)
    nf = pl.num_programs(1)
    n_assign = tok_ref.shape[0]

    @pl.when(i < n_used_ref[0])
    def _():
        @pl.when(f == 0)
        def _():
            base = tile_base_ref[i]

            def issue(r, c):
                tok = tok_ref[jnp.minimum(base + r, n_assign - 1)]
                pltpu.make_async_copy(h_hbm.at[pl.ds(tok, 1), :], hbuf.at[pl.ds(r, 1), :], sem).start()
                return c
            lax.fori_loop(0, MOE_TM, issue, 0)
            pltpu.make_async_copy(h_hbm.at[pl.ds(0, MOE_TM), :], hbuf, sem).wait()
            hb16[...] = hbuf[...].astype(BF16)
            acc_ref[...] = jnp.zeros_like(acc_ref)

        h = hb16[...]
        x_glu = jnp.minimum(_dot(h, wg_ref[...]) + bg_ref[...], SWIGLU_LIMIT)
        x_lin = jnp.clip(_dot(h, wl_ref[...]) + bl_ref[...], -SWIGLU_LIMIT, SWIGLU_LIMIT)
        act = x_glu * _sigmoid(SWIGLU_ALPHA * x_glu) * (x_lin + 1.0)
        acc_ref[...] += _dot(act.astype(BF16), wd_ref[...])

        @pl.when(f == nf - 1)
        def _():
            y_ref[...] = acc_ref[...] + bd_ref[...]

    @pl.when((i >= n_used_ref[0]) & (f == 0))
    def _():
        y_ref[...] = jnp.zeros_like(y_ref)


def _gmm(tile_e, tile_base, n_used, sorted_tok, h3, w_up, b_up, w_down, b_down):
    n_tiles = tile_e.shape[0]
    nf = D_FF // MOE_TF

    def live(i, f, te, nu):
        ok = i < nu[0]
        return jnp.where(ok, te[i], te[jnp.maximum(nu[0] - 1, 0)]), jnp.where(ok, f, nf - 1)

    def wg_map(i, f, te, tb, nu, st):
        e, ff = live(i, f, te, nu)
        return (e, 0, ff)

    def wl_map(i, f, te, tb, nu, st):
        e, ff = live(i, f, te, nu)
        return (e, 0, nf + ff)

    def wd_map(i, f, te, tb, nu, st):
        e, ff = live(i, f, te, nu)
        return (e, ff, 0)

    def bd_map(i, f, te, tb, nu, st):
        e, _ = live(i, f, te, nu)
        return (e, 0, 0)

    grid_spec = pltpu.PrefetchScalarGridSpec(
        num_scalar_prefetch=4,
        grid=(n_tiles, nf),
        in_specs=[pl.BlockSpec(memory_space=pl.ANY),
                  pl.BlockSpec((None, D_MODEL, MOE_TF), wg_map),
                  pl.BlockSpec((None, D_MODEL, MOE_TF), wl_map),
                  pl.BlockSpec((None, 1, MOE_TF), wg_map),
                  pl.BlockSpec((None, 1, MOE_TF), wl_map),
                  pl.BlockSpec((None, MOE_TF, D_MODEL), wd_map),
                  pl.BlockSpec((None, 1, D_MODEL), bd_map)],
        out_specs=pl.BlockSpec((MOE_TM, D_MODEL), lambda i, f, te, tb, nu, st: (i, 0)),
        scratch_shapes=[pltpu.VMEM((MOE_TM, D_MODEL), F32), pltpu.VMEM((MOE_TM, D_MODEL), BF16),
                        pltpu.VMEM((MOE_TM, D_MODEL), F32), pltpu.SemaphoreType.DMA(())],
    )
    return pl.pallas_call(
        _gmm_body,
        grid_spec=grid_spec,
        out_shape=jax.ShapeDtypeStruct((n_tiles * MOE_TM, D_MODEL), F32),
        compiler_params=_cparams(("arbitrary", "arbitrary")),
        name="moe_gmm",
    )(tile_e, tile_base, n_used, sorted_tok, h3, w_up, w_up, b_up, b_up, w_down, b_down)


CMB_TT = 128


def _combine_body(pos_ref, y_hbm, gate_ref, x_ref, g_ref, o_ref, buf, sem):
    i = pl.program_id(0)

    def issue(r, c):
        for k in range(TOP_K):
            p = pos_ref[(i * CMB_TT + r) * TOP_K + k]
            pltpu.make_async_copy(y_hbm.at[pl.ds(p, 1), :], buf.at[k, pl.ds(r, 1), :], sem).start()
        return c
    lax.fori_loop(0, CMB_TT, issue, 0)
    for k in range(TOP_K):
        pltpu.make_async_copy(y_hbm.at[pl.ds(0, CMB_TT), :], buf.at[k], sem).wait()
    gate = gate_ref[...]
    y = x_ref[...]
    for k in range(TOP_K):
        y = y + gate[:, k:k + 1] * buf[k]
    o_ref[...] = _rms(y, g_ref[...])


def _combine(pos, y_rows, gate, x2, g_final):
    T = x2.shape[0]
    grid_spec = pltpu.PrefetchScalarGridSpec(
        num_scalar_prefetch=1,
        grid=(T // CMB_TT,),
        in_specs=[pl.BlockSpec(memory_space=pl.ANY),
                  pl.BlockSpec((CMB_TT, LANES), lambda i, p: (i, 0)),
                  pl.BlockSpec((CMB_TT, D_MODEL), lambda i, p: (i, 0)),
                  pl.BlockSpec((1, D_MODEL), lambda i, p: (0, 0))],
        out_specs=pl.BlockSpec((CMB_TT, D_MODEL), lambda i, p: (i, 0)),
        scratch_shapes=[pltpu.VMEM((TOP_K, CMB_TT, D_MODEL), F32), pltpu.SemaphoreType.DMA(())],
    )
    return pl.pallas_call(
        _combine_body,
        grid_spec=grid_spec,
        out_shape=jax.ShapeDtypeStruct((T, D_MODEL), F32),
        compiler_params=_cparams(("arbitrary",)),
        name="moe_combine",
    )(pos, y_rows, gate, x2, g_final)


def _route(top_idx):
    T = top_idx.shape[0]
    n_assign = T * TOP_K
    flat_e = top_idx.reshape(-1)
    order = jnp.argsort(flat_e, stable=True)
    onehot = (flat_e[:, None] == jnp.arange(N_EXPERTS, dtype=jnp.int32)[None, :]).astype(jnp.int32)
    running = jnp.cumsum(onehot, axis=0)
    counts = running[-1]
    within = jnp.sum(onehot * (running - 1), axis=1)
    padded = (counts + MOE_TM - 1) // MOE_TM * MOE_TM
    start = jnp.cumsum(counts) - counts
    pend = jnp.cumsum(padded)
    pstart = pend - padded
    n_tiles = -(-n_assign // MOE_TM) + N_EXPERTS
    tile_row = jnp.arange(n_tiles, dtype=jnp.int32) * MOE_TM
    tile_e = jnp.minimum(jnp.sum(tile_row[:, None] >= pend[None, :], axis=1), N_EXPERTS - 1).astype(jnp.int32)
    tile_base = (start[tile_e] + tile_row - pstart[tile_e]).astype(jnp.int32)
    n_used = (pend[-1] // MOE_TM).astype(jnp.int32).reshape(1)
    sorted_tok = (order // TOP_K).astype(jnp.int32)
    pos = (pstart[flat_e] + within).astype(jnp.int32)
    return tile_e, tile_base, n_used, sorted_tok, pos


def _layer(x, mem, g_mix, w_in, conv_w, conv_b, conv_ln_g, conv_ln_b, pe_cmp, w_cmp1, b_cmp1, w_cmp2, b_cmp2,
           w_out, g_x, g_mem, w_xq, w_xk, w_xv, w_xo, g_ffn, w_router, b_router, w_up, b_up, w_down, b_down):
    B, S, D = x.shape
    T = B * S
    x2d = x.reshape(T, D)
    row = lambda v: v.reshape(1, -1)

    w_pad = jnp.pad(w_in, ((0, 0), (0, N_IN_PAD - w_in.shape[1]))).astype(BF16)
    uc, q, kv, gt = _inproj(x2d, row(g_mix), w_pad)

    y_conv = _conv(uc, B, S, conv_w, row(conv_b), row(conv_ln_g), row(conv_ln_b))

    kv16 = kv[:, :2 * KV_WIDTH].reshape(B, S, 2, N_KV_HEADS, HEAD_DIM).transpose(2, 0, 3, 1, 4) \
        .reshape(2, B, N_KV_HEADS, S // CMP_STRIDE, CMP_STRIDE * HEAD_DIM)
    cmp_kv = _compress(kv16, pe_cmp.reshape(2, 1, CMP_LEN * HEAD_DIM), w_cmp1.astype(BF16),
                       b_cmp1.reshape(2, 1, CMP_HIDDEN), w_cmp2.astype(BF16), b_cmp2.reshape(2, 1, HEAD_DIM))
    gates = gt[:, :N_GATE].reshape(B, S, 3, N_KV_HEADS, GQA_GROUP).transpose(0, 3, 1, 2, 4) \
        .reshape(B, N_KV_HEADS, S, 3 * GQA_GROUP)
    y_nsa = _nsa(q, cmp_kv, kv, gates, B, S)

    M = mem.shape[1]
    mk, mv = _memkv(mem.reshape(B * M, D), row(g_mem), w_xk.astype(BF16), w_xv.astype(BF16))
    w_r = jnp.pad(w_router, ((0, 0), (0, LANES - N_EXPERTS)))
    b_r = jnp.pad(b_router, (0, LANES - N_EXPERTS), constant_values=NEG).reshape(1, LANES)
    x2, h3, idx, gate = _mid(x2d, y_conv, y_nsa, w_out.astype(BF16), row(g_x), w_xq.astype(BF16), mk, mv,
                             w_xo.astype(BF16), row(g_ffn), w_r, b_r, B, S)

    tile_e, tile_base, n_used, sorted_tok, pos = _route(idx[:, :TOP_K])
    y_rows = _gmm(tile_e, tile_base, n_used, sorted_tok, h3, w_up.astype(BF16),
                  b_up.reshape(N_EXPERTS, 1, 2 * D_FF), w_down.astype(BF16), b_down.reshape(N_EXPERTS, 1, D_MODEL))
    return x2, y_rows, pos, gate


def kernel(x, mem, g_mix, w_in, conv_w, conv_b, conv_ln_g, conv_ln_b, pe_cmp, w_cmp1, b_cmp1, w_cmp2, b_cmp2, w_out, g_x, g_mem, w_xq, w_xk, w_xv, w_xo, g_ffn, w_router, b_router, w_up, b_up, w_down, b_down, g_final):
    B, S, D = x.shape
    assert g_mix.shape[0] == 1, "single-layer block"
    x2, y_rows, pos, gate = _layer(x, mem, g_mix[0], w_in[0], conv_w[0], conv_b[0], conv_ln_g[0], conv_ln_b[0],
                                   pe_cmp[0], w_cmp1[0], b_cmp1[0], w_cmp2[0], b_cmp2[0], w_out[0], g_x[0],
                                   g_mem[0], w_xq[0], w_xk[0], w_xv[0], w_xo[0], g_ffn[0], w_router[0],
                                   b_router[0], w_up[0], b_up[0], w_down[0], b_down[0])
    out = _combine(pos, y_rows, gate, x2, g_final.reshape(1, D))
    return out.reshape(B, S, D)
```

```python
import functools

import numpy as np
import jax
import jax.numpy as jnp
from jax import lax
from jax.experimental import pallas as pl
from jax.experimental.pallas import tpu as pltpu

F32 = jnp.float32
BF16 = jnp.bfloat16

D_MODEL = 2048
CONV_CH = 1024
CONV_WIDTH = 31
N_HEADS = 16
HEAD_DIM = 64
N_KV_HEADS = 4
GQA_GROUP = 4
KV_WIDTH = N_KV_HEADS * HEAD_DIM
CMP_LEN = 32
CMP_STRIDE = 16
CMP_HIDDEN = 256
SEL_LEN = 64
SEL_TOPK = 16
N_FORCED = 3
WIN = 512
Q_BLOCK = 128
FORCE_SCORE = 1.0e4
N_XHEADS = 4
XHEAD_DIM = 128
XW = N_XHEADS * XHEAD_DIM
N_EXPERTS = 32
TOP_K = 4
D_FF = D_MODEL
SWIGLU_LIMIT = 7.0
SWIGLU_ALPHA = 1.702
NORM_EPS = 1e-5
NEG = -1e30
TAKEN = -3e38
LANES = 128
N_GATE = 3 * N_HEADS
COL_Q = 2 * CONV_CH
COL_KV = COL_Q + N_HEADS * HEAD_DIM
COL_GATE = COL_KV + 6 * KV_WIDTH
N_IN_PAD = COL_GATE + LANES
VMEM_LIMIT = 56 * 1024 * 1024


def _cparams(sem):
    return pltpu.CompilerParams(dimension_semantics=sem, vmem_limit_bytes=VMEM_LIMIT)


def _rms(x, g):
    return x * lax.rsqrt(jnp.mean(x * x, axis=-1, keepdims=True) + NORM_EPS) * g


def _sigmoid(x):
    return 1.0 / (1.0 + jnp.exp(-x))


def _dot(a, b):
    return jnp.dot(a, b, preferred_element_type=F32)


def _dot_nt(a, b):
    return lax.dot_general(a, b, (((1,), (1,)), ((), ())), preferred_element_type=F32)


def _split3(x):
    hi = x.astype(BF16)
    r1 = x - hi.astype(F32)
    mid = r1.astype(BF16)
    lo = (r1 - mid.astype(F32)).astype(BF16)
    return hi, mid, lo


def _inproj_body(x_ref, g_ref, w_ref, uc_ref, q_ref, kv_ref, gt_ref):
    h = _rms(x_ref[...], g_ref[...]).astype(BF16)
    uc_ref[...] = _dot(h, w_ref[:, 0:COL_Q])
    q_ref[...] = (_dot(h, w_ref[:, COL_Q:COL_KV]) * (HEAD_DIM ** -0.5)).astype(BF16)
    kv_ref[...] = _dot(h, w_ref[:, COL_KV:COL_GATE]).astype(BF16)
    gt_ref[...] = _dot(h, w_ref[:, COL_GATE:N_IN_PAD])


def _inproj(x2d, g, w_pad, tm=256):
    T = x2d.shape[0]
    return pl.pallas_call(
        _inproj_body,
        grid=(T // tm,),
        in_specs=[pl.BlockSpec((tm, D_MODEL), lambda i: (i, 0)),
                  pl.BlockSpec((1, D_MODEL), lambda i: (0, 0)),
                  pl.BlockSpec((D_MODEL, N_IN_PAD), lambda i: (0, 0))],
        out_specs=[pl.BlockSpec((tm, COL_Q), lambda i: (i, 0)),
                   pl.BlockSpec((tm, COL_KV - COL_Q), lambda i: (i, 0)),
                   pl.BlockSpec((tm, COL_GATE - COL_KV), lambda i: (i, 0)),
                   pl.BlockSpec((tm, LANES), lambda i: (i, 0))],
        out_shape=[jax.ShapeDtypeStruct((T, COL_Q), F32),
                   jax.ShapeDtypeStruct((T, COL_KV - COL_Q), BF16),
                   jax.ShapeDtypeStruct((T, COL_GATE - COL_KV), BF16),
                   jax.ShapeDtypeStruct((T, LANES), F32)],
        compiler_params=_cparams(("parallel",)),
        name="inproj",
    )(x2d, g, w_pad)


CONV_HALO = 32


def _conv_body(ua_ref, ug_ref, ha_ref, hg_ref, w_ref, b_ref, lg_ref, lb_ref, o_ref, buf_ref, *, ts):
    i = pl.program_id(1)
    halo = ha_ref[...] * _sigmoid(hg_ref[...])
    buf_ref[0:CONV_HALO, :] = jnp.where(i > 0, halo, 0.0)
    buf_ref[CONV_HALO:CONV_HALO + ts, :] = ua_ref[...] * _sigmoid(ug_ref[...])
    off = CONV_HALO - (CONV_WIDTH - 1)
    acc = jnp.zeros((ts, CONV_CH), F32) + b_ref[...]
    for j in range(CONV_WIDTH):
        acc = acc + w_ref[j:j + 1, :] * buf_ref[off + j:off + j + ts, :]
    mu = jnp.mean(acc, axis=-1, keepdims=True)
    d = acc - mu
    var = jnp.mean(d * d, axis=-1, keepdims=True)
    yn = d * lax.rsqrt(var + NORM_EPS) * lg_ref[...] + lb_ref[...]
    o_ref[...] = (yn * _sigmoid(yn)).astype(o_ref.dtype)


def _conv(uc, B, S, w, b, lg, lb, ts=256):
    nT = S // ts
    r = ts // CONV_HALO
    cur = lambda col: (lambda bb, i: (bb * nT + i, col))
    prev = lambda col: (lambda bb, i: (jnp.maximum((bb * nT + i) * r - 1, 0), col))
    vec = lambda n: pl.BlockSpec((n, CONV_CH), lambda bb, i: (0, 0))
    return pl.pallas_call(
        functools.partial(_conv_body, ts=ts),
        grid=(B, nT),
        in_specs=[pl.BlockSpec((ts, CONV_CH), cur(0)), pl.BlockSpec((ts, CONV_CH), cur(1)),
                  pl.BlockSpec((CONV_HALO, CONV_CH), prev(0)), pl.BlockSpec((CONV_HALO, CONV_CH), prev(1)),
                  vec(CONV_WIDTH), vec(1), vec(1), vec(1)],
        out_specs=pl.BlockSpec((ts, CONV_CH), lambda bb, i: (bb * nT + i, 0)),
        out_shape=jax.ShapeDtypeStruct((B * S, CONV_CH), BF16),
        scratch_shapes=[pltpu.VMEM((CONV_HALO + ts, CONV_CH), F32)],
        compiler_params=_cparams(("parallel", "parallel")),
        name="conv",
    )(uc, uc, uc, uc, w, b, lg, lb)


N_BIAS = 6


def _pos_columns(pos_hi, pos_lo):
    n = pos_hi.shape[0]
    cols = np.zeros((n, LANES), np.float32)
    for half in (0, HEAD_DIM):
        for k in range(N_BIAS):
            cols[:, half + k] = pos_hi if k % 2 == 0 else pos_lo
    return cols


def _slope_columns():
    sl = jnp.asarray(2.0 ** (-8.0 * np.arange(1, N_HEADS + 1) / N_HEADS), F32)
    parts = _split3(sl)
    cols = jnp.zeros((N_HEADS, LANES), BF16)
    for half in (0, HEAD_DIM):
        for k in range(N_BIAS):
            cols = cols.at[:, half + k].set(parts[k // 2])
    cols = cols.reshape(N_KV_HEADS, GQA_GROUP, 1, LANES)
    return jnp.broadcast_to(cols, (N_KV_HEADS, GQA_GROUP, Q_BLOCK, LANES)).reshape(N_KV_HEADS, GQA_GROUP * Q_BLOCK, LANES)


def _compress_body(x_ref, pe_ref, w1_ref, b1_ref, w2_ref, b2_ref, c_ref, o_ref):
    half = (CMP_LEN // 2) * HEAD_DIM
    x = x_ref[...]
    first = _dot(x, w1_ref[0:half, :])
    second = _dot(x, w1_ref[half:2 * half, :])
    n = x.shape[0]
    second = pltpu.roll(second, n - 1, 0)
    pe8 = jnp.broadcast_to(pe_ref[...], (8, 2 * half)).astype(BF16)
    c1 = _dot(pe8, w1_ref[...])[0:1, :] + b1_ref[...]
    hid = first + second + c1
    act = (hid * _sigmoid(hid)).astype(BF16)
    o_ref[...] = (_dot(act, w2_ref[...]) + b2_ref[...] + c_ref[...]).astype(o_ref.dtype)


def _compress(kv16, pe, w1, b1, w2, b2):
    _, B, H, NC, W = kv16.shape
    c = np.arange(NC)
    consts = np.zeros((2, NC, LANES), np.float32)
    consts[0] = _pos_columns((c // 16) * 256.0, (c % 16) * 16.0)
    consts[0, :, :HEAD_DIM] = 0.0
    consts[1, :, HEAD_DIM:] = 1.0
    w2p = jnp.pad(w2, ((0, 0), (0, 0), (0, LANES - HEAD_DIM)))
    b2p = jnp.pad(b2, ((0, 0), (0, 0), (0, LANES - HEAD_DIM)))
    sel = lambda *rest: (lambda a, bb, h: (a,) + rest)
    return pl.pallas_call(
        _compress_body,
        grid=(2, B, H),
        in_specs=[pl.BlockSpec((None, None, None, NC, W), lambda a, bb, h: (a, bb, h, 0, 0)),
                  pl.BlockSpec((None, 1, W * 2), sel(0, 0)),
                  pl.BlockSpec((None, W * 2, CMP_HIDDEN), sel(0, 0)),
                  pl.BlockSpec((None, 1, CMP_HIDDEN), sel(0, 0)),
                  pl.BlockSpec((None, CMP_HIDDEN, LANES), sel(0, 0)),
                  pl.BlockSpec((None, 1, LANES), sel(0, 0)),
                  pl.BlockSpec((None, NC, LANES), sel(0, 0))],
        out_specs=pl.BlockSpec((None, None, None, NC, LANES), lambda a, bb, h: (a, bb, h, 0, 0)),
        out_shape=jax.ShapeDtypeStruct((2, B, H, NC, LANES), BF16),
        compiler_params=_cparams(("parallel", "parallel", "parallel")),
        name="compress",
    )(kv16, pe, w1, b1, w2p, b2p, jnp.asarray(consts))


NSA_TK = 512


def _nsa_body(q_ref, kc_ref, vc_ref, ks_ref, vs_ref, kw_ref, vw_ref, gate_ref, ovl_ref, hot_ref, pos_ref,
              slope_ref, o_ref, *, n_sel, n_cmp):
    par = pl.program_id(1) % 2
    i = pl.program_id(2)
    q0 = i * Q_BLOCK
    R = GQA_GROUP * Q_BLOCK
    lane = lax.broadcasted_iota(jnp.int32, (1, LANES), 1)
    low = lane < HEAD_DIM
    mine = (lane // HEAD_DIM) == par
    t = q0 + (lax.broadcasted_iota(jnp.int32, (R, 1), 0) & (Q_BLOCK - 1))

    qb = q_ref[...]
    in_low, in_mine = [], []
    for g in range(GQA_GROUP):
        col = qb[:, LANES * (g // 2):LANES * (g // 2 + 1)]
        swapped = pltpu.roll(col, HEAD_DIM, 1)
        in_low.append(col if g % 2 == 0 else swapped)
        in_mine.append(jnp.where((g % 2) == par, col, swapped))
    slope_cols = slope_ref[...]
    q_low = jnp.where(low, jnp.concatenate(in_low, axis=0), slope_cols)
    q_par = jnp.where(mine, jnp.concatenate(in_mine, axis=0), slope_cols)

    nc = kc_ref.shape[0]
    c_ix = lax.broadcasted_iota(jnp.int32, (1, nc), 1)
    mask_c = (t >= c_ix * CMP_STRIDE + (CMP_LEN - 1)) & (c_ix < n_cmp)
    s_c = jnp.where(mask_c, _dot_nt(q_low, kc_ref[...]), NEG)
    e_c = jnp.exp(s_c - jnp.max(s_c, axis=-1, keepdims=True))
    d_c = jnp.sum(e_c, axis=-1, keepdims=True)
    p_c = e_c * jnp.where(t >= CMP_LEN - 1, 1.0 / d_c, 0.0)
    o_c = _dot(p_c.astype(BF16), vc_ref[...])

    pc_sum = p_c[0:Q_BLOCK]
    for g in range(1, GQA_GROUP):
        pc_sum = pc_sum + p_c[g * Q_BLOCK:(g + 1) * Q_BLOCK]
    ovl = ovl_ref[...]
    imp = sum(_dot_nt(ovl, part) for part in _split3(pc_sum))
    j_ix = lax.broadcasted_iota(jnp.int32, (n_sel, 1), 0)
    j_f = j_ix.astype(F32)
    blk_t = (q0 + lax.broadcasted_iota(jnp.int32, (1, Q_BLOCK), 1)) // SEL_LEN
    forced = (j_ix == 0) | (j_ix == blk_t) | (j_ix == blk_t - 1)
    score = jnp.where((j_ix <= blk_t) & ~forced, imp, -FORCE_SCORE)
    sel = forced
    for _ in range(min(SEL_TOPK, n_sel) - N_FORCED):
        m = jnp.max(score, axis=0, keepdims=True)
        first = jnp.min(jnp.where(score == m, j_f, float(n_sel)), axis=0, keepdims=True)
        hit = j_f == first
        sel = sel | (hit & (m > -0.5 * FORCE_SCORE))
        score = jnp.where(hit, TAKEN, score)
    sel_bias = jnp.where(sel, 0.0, NEG).T.astype(BF16)
    q_sel = jnp.concatenate([q_par, jnp.concatenate([sel_bias] * GQA_GROUP, axis=0)], axis=1)

    def k_tile(k_ref, start, size):
        return jnp.where(mine, k_ref[pl.ds(start, size), :], pos_ref[pl.ds(start, size), :])

    def v_tile(v_ref, start, size):
        return jnp.where(mine, v_ref[pl.ds(start, size), :], jnp.ones((), BF16))

    span = WIN + Q_BLOCK
    w0 = pl.multiple_of(jnp.maximum(q0 - WIN, 0), Q_BLOCK)
    dist_w = t - (w0 + lax.broadcasted_iota(jnp.int32, (1, span), 1))
    mask_w = dist_w.astype(jnp.uint32) < WIN
    s_w = jnp.where(mask_w, _dot_nt(q_par, k_tile(kw_ref, w0, span)), NEG)
    e_w = jnp.exp(s_w - jnp.max(s_w, axis=-1, keepdims=True))
    acc_w = _dot(e_w.astype(BF16), v_tile(vw_ref, w0, span))

    def scores(k0):
        k_aug = jnp.concatenate([k_tile(ks_ref, k0, NSA_TK), hot_ref[pl.ds(k0, NSA_TK), :]], axis=1)
        return _dot_nt(q_sel, k_aug)

    def absorb(s, k0, m_run, acc):
        m_new = jnp.maximum(m_run, jnp.max(s, axis=-1, keepdims=True))
        p = jnp.exp(s - m_new)
        acc = jnp.exp(m_run - m_new) * acc + _dot(p.astype(BF16), v_tile(vs_ref, k0, NSA_TK))
        return m_new, acc

    def sweep(kt, carry):
        m_run, acc, s = carry
        k0 = pl.multiple_of(kt * NSA_TK, NSA_TK)
        s_next = scores(k0 + NSA_TK)
        m_run, acc = absorb(s, k0, m_run, acc)
        return m_run, acc, s_next

    n_full = q0 // NSA_TK
    carry = (jnp.full((R, 1), NEG, F32), jnp.zeros((R, LANES), F32), scores(0))
    m_run, acc_s, s_last = lax.fori_loop(0, n_full, sweep, carry)
    k_last = pl.multiple_of(n_full * NSA_TK, NSA_TK)
    s_last = jnp.where(t >= k_last + lax.broadcasted_iota(jnp.int32, (1, NSA_TK), 1), s_last, NEG)
    _, acc_s = absorb(s_last, k_last, m_run, acc_s)

    def normalised(acc):
        return acc / jnp.where(mine, pltpu.roll(acc, HEAD_DIM, 1), 1.0)

    o_sw_all = (normalised(acc_s), normalised(acc_w))
    gate = _sigmoid(gate_ref[...])
    placed = []
    for g in range(GQA_GROUP):
        rows = slice(g * Q_BLOCK, (g + 1) * Q_BLOCK)
        o_sw = (gate[:, GQA_GROUP + g:GQA_GROUP + g + 1] * o_sw_all[0][rows]
                + gate[:, 2 * GQA_GROUP + g:2 * GQA_GROUP + g + 1] * o_sw_all[1][rows])
        o_cg = gate[:, g:g + 1] * o_c[rows]
        if g % 2 == 1:
            o_cg = pltpu.roll(o_cg, HEAD_DIM, 1)
        o_sw = jnp.where((g % 2) == par, o_sw, pltpu.roll(o_sw, HEAD_DIM, 1))
        placed.append(o_cg + o_sw)
    out = [jnp.where(low, placed[2 * c], placed[2 * c + 1]) for c in range(GQA_GROUP // 2)]
    o_ref[...] = jnp.concatenate(out, axis=1).astype(o_ref.dtype)


def _selection_overlap_t(n_cmp_pad, n_cmp, n_sel):
    cs = np.arange(n_cmp_pad) * CMP_STRIDE
    ce = cs + CMP_LEN - 1
    js = np.arange(n_sel) * SEL_LEN
    m = (cs[None, :] <= js[:, None] + SEL_LEN - 1) & (ce[None, :] >= js[:, None])
    m &= (np.arange(n_cmp_pad) < n_cmp)[None, :]
    return m.astype(np.float32)


def _nsa(q, cmp_kv, kv, gates, B, S):
    n_sel = S // SEL_LEN
    assert n_sel >= SEL_TOPK and NSA_TK % Q_BLOCK == 0 and S % NSA_TK == 0 and S >= WIN + Q_BLOCK
    n_cmp = (S - CMP_LEN) // CMP_STRIDE + 1
    nc = S // CMP_STRIDE
    nQ = S // Q_BLOCK
    T = B * S
    ovl = jnp.asarray(_selection_overlap_t(nc, n_cmp, n_sel), BF16)
    pos = np.arange(S)
    hot = jnp.asarray((pos[:, None] // SEL_LEN) == np.arange(n_sel)[None, :], BF16)
    pos_cols = jnp.asarray(_pos_columns((pos // 64) * 64.0, (pos % 64) * 1.0), BF16)
    cmp_spec = lambda a: pl.BlockSpec((None, None, None, nc, LANES), lambda bb, h, i: (a, bb, h, 0, 0))
    kv_spec = lambda a: pl.BlockSpec((S, LANES), lambda bb, h, i: (bb, 2 * a + h // 2))
    const = lambda shape: pl.BlockSpec(shape, lambda bb, h, i: (0,) * len(shape))
    return pl.pallas_call(
        functools.partial(_nsa_body, n_sel=n_sel, n_cmp=n_cmp),
        grid=(B, N_KV_HEADS, nQ),
        in_specs=[pl.BlockSpec((Q_BLOCK, GQA_GROUP * HEAD_DIM), lambda bb, h, i: (bb * nQ + i, h)),
                  cmp_spec(0), cmp_spec(1), kv_spec(2), kv_spec(3), kv_spec(4), kv_spec(5),
                  pl.BlockSpec((None, None, Q_BLOCK, 3 * GQA_GROUP), lambda bb, h, i: (bb, h, i, 0)),
                  const((n_sel, nc)), const((S, n_sel)), const((S, LANES)),
                  pl.BlockSpec((None, GQA_GROUP * Q_BLOCK, LANES), lambda bb, h, i: (h, 0, 0))],
        out_specs=pl.BlockSpec((Q_BLOCK, GQA_GROUP * HEAD_DIM), lambda bb, h, i: (bb * nQ + i, h)),
        out_shape=jax.ShapeDtypeStruct((T, N_HEADS * HEAD_DIM), BF16),
        compiler_params=_cparams(("parallel", "parallel", "arbitrary")),
        name="nsa",
    )(q, cmp_kv, cmp_kv, kv, kv, kv, kv, gates, ovl, hot, pos_cols, _slope_columns())


def _memkv_body(m_ref, g_ref, wk_ref, wv_ref, k_ref, v_ref):
    h = _rms(m_ref[...], g_ref[...]).astype(BF16)
    k_ref[...] = _dot(h, wk_ref[...]).astype(BF16)
    v_ref[...] = _dot(h, wv_ref[...]).astype(BF16)


def _memkv(mem2d, g, wk, wv, tm=256):
    R = mem2d.shape[0]
    wspec = pl.BlockSpec((D_MODEL, XW), lambda i: (0, 0))
    ospec = pl.BlockSpec((tm, XW), lambda i: (i, 0))
    return pl.pallas_call(
        _memkv_body,
        grid=(R // tm,),
        in_specs=[pl.BlockSpec((tm, D_MODEL), lambda i: (i, 0)), pl.BlockSpec((1, D_MODEL), lambda i: (0, 0)),
                  wspec, wspec],
        out_specs=[ospec, ospec],
        out_shape=[jax.ShapeDtypeStruct((R, XW), BF16)] * 2,
        compiler_params=_cparams(("parallel",)),
        name="memkv",
    )(mem2d, g, wk, wv)


def _mid_body(x_ref, yc_ref, yn_ref, wo_ref, gx_ref, wq_ref, mk_ref, mv_ref, wxo_ref, gf_ref, wr_ref, br_ref,
              x2_ref, h3_ref, idx_ref, gate_ref):
    x1 = x_ref[...] + _dot(yc_ref[...], wo_ref[0:CONV_CH, :]) + _dot(yn_ref[...], wo_ref[CONV_CH:2 * CONV_CH, :])
    h = _rms(x1, gx_ref[...]).astype(BF16)
    q = (_dot(h, wq_ref[...]) * (XHEAD_DIM ** -0.5)).astype(BF16)
    heads = []
    for hd in range(N_XHEADS):
        cols = slice(hd * XHEAD_DIM, (hd + 1) * XHEAD_DIM)
        s = _dot_nt(q[:, cols], mk_ref[:, cols])
        e = jnp.exp(s - jnp.max(s, axis=-1, keepdims=True))
        o = _dot(e.astype(BF16), mv_ref[:, cols]) / jnp.sum(e, axis=-1, keepdims=True)
        heads.append(o.astype(BF16))
    x2 = x1 + _dot(jnp.concatenate(heads, axis=-1), wxo_ref[...])
    x2_ref[...] = x2
    h3 = _rms(x2, gf_ref[...])
    h3_ref[...] = h3
    h_hi, h_mid, _ = _split3(h3)
    w_hi, w_mid, _ = _split3(wr_ref[...])
    logits = _dot(h_hi, w_hi) + _dot(h_hi, w_mid) + _dot(h_mid, w_hi) + br_ref[...]
    lane = lax.broadcasted_iota(jnp.int32, (1, LANES), 1)
    lane_f = lane.astype(F32)
    idx_out = jnp.zeros(logits.shape, F32)
    val_out = jnp.full(logits.shape, NEG, F32)
    for k in range(TOP_K):
        m = jnp.max(logits, axis=-1, keepdims=True)
        first = jnp.min(jnp.where(logits == m, lane_f, float(LANES)), axis=-1, keepdims=True)
        idx_out = jnp.where(lane == k, first, idx_out)
        val_out = jnp.where(lane == k, m, val_out)
        logits = jnp.where(lane_f == first, TAKEN, logits)
    e = jnp.where(lane < TOP_K, jnp.exp(val_out - jnp.max(val_out, axis=-1, keepdims=True)), 0.0)
    idx_ref[...] = idx_out.astype(jnp.int32)
    gate_ref[...] = e / jnp.sum(e, axis=-1, keepdims=True)


def _mid(x2d, yc, yn, w_out, g_x, w_xq, mk, mv, w_xo, g_ffn, w_r, b_r, B, S, tm=256):
    T = x2d.shape[0]
    nT = S // tm
    M = mk.shape[0] // B
    row = lambda w: pl.BlockSpec((tm, w), lambda i: (i, 0))
    full = lambda a: pl.BlockSpec(a.shape, lambda i: (0,) * a.ndim)
    memspec = pl.BlockSpec((M, XW), lambda i: (i // nT, 0))
    return pl.pallas_call(
        _mid_body,
        grid=(T // tm,),
        in_specs=[row(D_MODEL), row(CONV_CH), row(CONV_CH), full(w_out), full(g_x), full(w_xq), memspec, memspec,
                  full(w_xo), full(g_ffn), full(w_r), full(b_r)],
        out_specs=[row(D_MODEL), row(D_MODEL), row(LANES), row(LANES)],
        out_shape=[jax.ShapeDtypeStruct((T, D_MODEL), F32), jax.ShapeDtypeStruct((T, D_MODEL), F32),
                   jax.ShapeDtypeStruct((T, LANES), jnp.int32), jax.ShapeDtypeStruct((T, LANES), F32)],
        compiler_params=_cparams(("parallel",)),
        name="mid",
    )(x2d, yc, yn, w_out, g_x, w_xq, mk, mv, w_xo, g_ffn, w_r, b_r)


MOE_TM = 512
MOE_TF = 256
MOE_ST = 4


def _gmm_body(se_ref, st0_ref, nrt_ref, sb_ref, nsu_ref, tok_ref, h_hbm, wg_ref, wl_ref, bg_ref, bl_ref, wd_ref,
              bd_ref, y_hbm, stage, hb16, acc_ref, sem_in, sem_out):
    s = pl.program_id(0)
    f = pl.program_id(1)
    nf = pl.num_programs(1)
    n_assign = tok_ref.shape[0]

    @pl.when(s < nsu_ref[0])
    def _():
        n_rt = nrt_ref[s]

        @pl.when(f == 0)
        def _():
            def gather_tile(r, c):
                base = sb_ref[s] + r * MOE_TM

                def issue(j, c2):
                    tok = tok_ref[jnp.minimum(base + j, n_assign - 1)]
                    pltpu.make_async_copy(h_hbm.at[pl.ds(tok, 1), :], stage.at[pl.ds(j, 1), :], sem_in).start()
                    return c2
                lax.fori_loop(0, MOE_TM, issue, 0, unroll=8)
                pltpu.make_async_copy(h_hbm.at[pl.ds(0, MOE_TM), :], stage, sem_in).wait()
                hb16[r] = stage[...].astype(BF16)
                acc_ref[r] = jnp.zeros((MOE_TM, D_MODEL), F32)
                return c
            lax.fori_loop(0, n_rt, gather_tile, 0)

        wg = wg_ref[...].astype(BF16)
        wl = wl_ref[...].astype(BF16)
        wd = wd_ref[...].astype(BF16)

        def row_tile(r, c):
            h = hb16[r]
            x_glu = jnp.minimum(_dot(h, wg) + bg_ref[...], SWIGLU_LIMIT)
            x_lin = jnp.clip(_dot(h, wl) + bl_ref[...], -SWIGLU_LIMIT, SWIGLU_LIMIT)
            act = x_glu * _sigmoid(SWIGLU_ALPHA * x_glu) * (x_lin + 1.0)
            acc_ref[r] += _dot(act.astype(BF16), wd)
            return c
        lax.fori_loop(0, n_rt, row_tile, 0)

        @pl.when(f == nf - 1)
        def _():
            def out_tile(r, c):
                stage[...] = acc_ref[r] + bd_ref[...]
                row0 = pl.multiple_of((st0_ref[s] + r) * MOE_TM, MOE_TM)
                cp = pltpu.make_async_copy(stage, y_hbm.at[pl.ds(row0, MOE_TM), :], sem_out)
                cp.start()
                cp.wait()
                return c
            lax.fori_loop(0, n_rt, out_tile, 0)


def _gmm(sup, sorted_tok, h3, w_up, b_up, w_down, b_down, n_rows):
    se, st0, nrt, sb, nsu = sup
    nf = D_FF // MOE_TF

    def live(s, f, se_, nsu_):
        ok = s < nsu_[0]
        return jnp.where(ok, se_[s], se_[jnp.maximum(nsu_[0] - 1, 0)]), jnp.where(ok, f, nf - 1)

    def wg_map(s, f, se_, st0_, nrt_, sb_, nsu_, tok_):
        e, ff = live(s, f, se_, nsu_)
        return (e, 0, ff)

    def wl_map(s, f, se_, st0_, nrt_, sb_, nsu_, tok_):
        e, ff = live(s, f, se_, nsu_)
        return (e, 0, nf + ff)

    def wd_map(s, f, se_, st0_, nrt_, sb_, nsu_, tok_):
        e, ff = live(s, f, se_, nsu_)
        return (e, ff, 0)

    def bd_map(s, f, se_, st0_, nrt_, sb_, nsu_, tok_):
        e, _ = live(s, f, se_, nsu_)
        return (e, 0, 0)

    grid_spec = pltpu.PrefetchScalarGridSpec(
        num_scalar_prefetch=6,
        grid=(se.shape[0], nf),
        in_specs=[pl.BlockSpec(memory_space=pl.ANY),
                  pl.BlockSpec((None, D_MODEL, MOE_TF), wg_map),
                  pl.BlockSpec((None, D_MODEL, MOE_TF), wl_map),
                  pl.BlockSpec((None, 1, MOE_TF), wg_map),
                  pl.BlockSpec((None, 1, MOE_TF), wl_map),
                  pl.BlockSpec((None, MOE_TF, D_MODEL), wd_map),
                  pl.BlockSpec((None, 1, D_MODEL), bd_map)],
        out_specs=pl.BlockSpec(memory_space=pl.ANY),
        scratch_shapes=[pltpu.VMEM((MOE_TM, D_MODEL), F32), pltpu.VMEM((MOE_ST, MOE_TM, D_MODEL), BF16),
                        pltpu.VMEM((MOE_ST, MOE_TM, D_MODEL), F32), pltpu.SemaphoreType.DMA(()),
                        pltpu.SemaphoreType.DMA(())],
    )
    return pl.pallas_call(
        _gmm_body,
        grid_spec=grid_spec,
        out_shape=jax.ShapeDtypeStruct((n_rows, D_MODEL), F32),
        compiler_params=_cparams(("arbitrary", "arbitrary")),
        name="moe_gmm",
    )(se, st0, nrt, sb, nsu, sorted_tok, h3, w_up, w_up, b_up, b_up, w_down, b_down)


CMB_TT = 128


def _combine_body(pos_ref, y_hbm, gate_ref, x_ref, g_ref, o_ref, buf, sem):
    i = pl.program_id(0)

    def issue(r, c):
        for k in range(TOP_K):
            p = pos_ref[(i * CMB_TT + r) * TOP_K + k]
            pltpu.make_async_copy(y_hbm.at[pl.ds(p, 1), :], buf.at[k, pl.ds(r, 1), :], sem).start()
        return c
    lax.fori_loop(0, CMB_TT, issue, 0, unroll=4)
    for k in range(TOP_K):
        pltpu.make_async_copy(y_hbm.at[pl.ds(0, CMB_TT), :], buf.at[k], sem).wait()
    gate = gate_ref[...]
    y = x_ref[...]
    for k in range(TOP_K):
        y = y + gate[:, k:k + 1] * buf[k]
    o_ref[...] = _rms(y, g_ref[...])


def _combine(pos, y_rows, gate, x2, g_final):
    T = x2.shape[0]
    grid_spec = pltpu.PrefetchScalarGridSpec(
        num_scalar_prefetch=1,
        grid=(T // CMB_TT,),
        in_specs=[pl.BlockSpec(memory_space=pl.ANY),
                  pl.BlockSpec((CMB_TT, LANES), lambda i, p: (i, 0)),
                  pl.BlockSpec((CMB_TT, D_MODEL), lambda i, p: (i, 0)),
                  pl.BlockSpec((1, D_MODEL), lambda i, p: (0, 0))],
        out_specs=pl.BlockSpec((CMB_TT, D_MODEL), lambda i, p: (i, 0)),
        scratch_shapes=[pltpu.VMEM((TOP_K, CMB_TT, D_MODEL), F32), pltpu.SemaphoreType.DMA(())],
    )
    return pl.pallas_call(
        _combine_body,
        grid_spec=grid_spec,
        out_shape=jax.ShapeDtypeStruct((T, D_MODEL), F32),
        compiler_params=_cparams(("arbitrary",)),
        name="moe_combine",
    )(pos, y_rows, gate, x2, g_final)


def _route(top_idx):
    T = top_idx.shape[0]
    n_assign = T * TOP_K
    flat_e = top_idx.reshape(-1)
    order = jnp.argsort(flat_e, stable=True)
    onehot = (flat_e[:, None] == jnp.arange(N_EXPERTS, dtype=jnp.int32)[None, :]).astype(jnp.int32)
    running = jnp.cumsum(onehot, axis=0)
    counts = running[-1]
    within = jnp.sum(onehot * (running - 1), axis=1)
    padded = (counts + MOE_TM - 1) // MOE_TM * MOE_TM
    start = jnp.cumsum(counts) - counts
    pend = jnp.cumsum(padded)
    pstart = pend - padded
    n_tiles = -(-n_assign // MOE_TM) + N_EXPERTS
    tiles_e = padded // MOE_TM
    supers_e = (tiles_e + MOE_ST - 1) // MOE_ST
    s_end = jnp.cumsum(supers_e)
    s_ix = jnp.arange(-(-n_tiles // MOE_ST) + N_EXPERTS, dtype=jnp.int32)
    s_e = jnp.minimum(jnp.sum(s_ix[:, None] >= s_end[None, :], axis=1), N_EXPERTS - 1).astype(jnp.int32)
    local = s_ix - (s_end - supers_e)[s_e]
    i32 = lambda v: v.astype(jnp.int32)
    sup = (s_e, i32(pstart[s_e] // MOE_TM + MOE_ST * local), i32(jnp.clip(tiles_e[s_e] - MOE_ST * local, 0, MOE_ST)),
           i32(start[s_e] + MOE_ST * MOE_TM * local), i32(s_end[-1]).reshape(1))
    sorted_tok = (order // TOP_K).astype(jnp.int32)
    pos = (pstart[flat_e] + within).astype(jnp.int32)
    return sup, sorted_tok, pos, n_tiles * MOE_TM


def _layer(x, mem, g_mix, w_in, conv_w, conv_b, conv_ln_g, conv_ln_b, pe_cmp, w_cmp1, b_cmp1, w_cmp2, b_cmp2,
           w_out, g_x, g_mem, w_xq, w_xk, w_xv, w_xo, g_ffn, w_router, b_router, w_up, b_up, w_down, b_down):
    B, S, D = x.shape
    T = B * S
    x2d = x.reshape(T, D)
    row = lambda v: v.reshape(1, -1)

    w_pad = jnp.pad(w_in, ((0, 0), (0, N_IN_PAD - w_in.shape[1]))).astype(BF16)
    uc, q, kv, gt = _inproj(x2d, row(g_mix), w_pad)

    y_conv = _conv(uc, B, S, conv_w, row(conv_b), row(conv_ln_g), row(conv_ln_b))

    kv16 = kv[:, :2 * KV_WIDTH].reshape(B, S, 2, N_KV_HEADS, HEAD_DIM).transpose(2, 0, 3, 1, 4) \
        .reshape(2, B, N_KV_HEADS, S // CMP_STRIDE, CMP_STRIDE * HEAD_DIM)
    cmp_kv = _compress(kv16, pe_cmp.reshape(2, 1, CMP_LEN * HEAD_DIM), w_cmp1.astype(BF16),
                       b_cmp1.reshape(2, 1, CMP_HIDDEN), w_cmp2.astype(BF16), b_cmp2.reshape(2, 1, HEAD_DIM))
    gates = gt[:, :N_GATE].reshape(B, S, 3, N_KV_HEADS, GQA_GROUP).transpose(0, 3, 1, 2, 4) \
        .reshape(B, N_KV_HEADS, S, 3 * GQA_GROUP)
    y_nsa = _nsa(q, cmp_kv, kv, gates, B, S)

    M = mem.shape[1]
    mk, mv = _memkv(mem.reshape(B * M, D), row(g_mem), w_xk.astype(BF16), w_xv.astype(BF16))
    w_r = jnp.pad(w_router, ((0, 0), (0, LANES - N_EXPERTS)))
    b_r = jnp.pad(b_router, (0, LANES - N_EXPERTS), constant_values=NEG).reshape(1, LANES)
    x2, h3, idx, gate = _mid(x2d, y_conv, y_nsa, w_out.astype(BF16), row(g_x), w_xq.astype(BF16), mk, mv,
                             w_xo.astype(BF16), row(g_ffn), w_r, b_r, B, S)

    sup, sorted_tok, pos, n_rows = _route(idx[:, :TOP_K])
    y_rows = _gmm(sup, sorted_tok, h3, w_up, b_up.reshape(N_EXPERTS, 1, 2 * D_FF), w_down,
                  b_down.reshape(N_EXPERTS, 1, D_MODEL), n_rows)
    return x2, y_rows, pos, gate


def kernel(x, mem, g_mix, w_in, conv_w, conv_b, conv_ln_g, conv_ln_b, pe_cmp, w_cmp1, b_cmp1, w_cmp2, b_cmp2, w_out, g_x, g_mem, w_xq, w_xk, w_xv, w_xo, g_ffn, w_router, b_router, w_up, b_up, w_down, b_down, g_final):
    B, S, D = x.shape
    assert g_mix.shape[0] == 1, "single-layer block"
    x2, y_rows, pos, gate = _layer(x, mem, g_mix[0], w_in[0], conv_w[0], conv_b[0], conv_ln_g[0], conv_ln_b[0],
                                   pe_cmp[0], w_cmp1[0], b_cmp1[0], w_cmp2[0], b_cmp2[0], w_out[0], g_x[0],
                                   g_mem[0], w_xq[0], w_xk[0], w_xv[0], w_xo[0], g_ffn[0], w_router[0],
                                   b_router[0], w_up[0], b_up[0], w_down[0], b_down[0])
    out = _combine(pos, y_rows, gate, x2, g_final.reshape(1, D))
    return out.reshape(B, S, D)
```

```python
import functools

import numpy as np
import jax
import jax.numpy as jnp
from jax import lax
from jax.experimental import pallas as pl
from jax.experimental.pallas import tpu as pltpu

F32 = jnp.float32
BF16 = jnp.bfloat16

D_MODEL = 2048
CONV_CH = 1024
CONV_WIDTH = 31
N_HEADS = 16
HEAD_DIM = 64
N_KV_HEADS = 4
GQA_GROUP = 4
KV_WIDTH = N_KV_HEADS * HEAD_DIM
CMP_LEN = 32
CMP_STRIDE = 16
CMP_HIDDEN = 256
SEL_LEN = 64
SEL_TOPK = 16
N_FORCED = 3
WIN = 512
Q_BLOCK = 256
FORCE_SCORE = 1.0e4
N_XHEADS = 4
XHEAD_DIM = 128
XW = N_XHEADS * XHEAD_DIM
N_EXPERTS = 32
TOP_K = 4
D_FF = D_MODEL
SWIGLU_LIMIT = 7.0
SWIGLU_ALPHA = 1.702
NORM_EPS = 1e-5
NEG = -1e30
TAKEN = -3e38
LANES = 128
N_GATE = 3 * N_HEADS
COL_Q = 2 * CONV_CH
COL_KV = COL_Q + N_HEADS * HEAD_DIM
COL_GATE = COL_KV + 6 * KV_WIDTH
N_IN_PAD = COL_GATE + LANES
VMEM_LIMIT = 56 * 1024 * 1024


def _cparams(sem):
    return pltpu.CompilerParams(dimension_semantics=sem, vmem_limit_bytes=VMEM_LIMIT)


def _rms(x, g):
    return x * lax.rsqrt(jnp.mean(x * x, axis=-1, keepdims=True) + NORM_EPS) * g


def _sigmoid(x):
    return 1.0 / (1.0 + jnp.exp(-x))


def _dot(a, b):
    return jnp.dot(a, b, preferred_element_type=F32)


def _dot_nt(a, b):
    return lax.dot_general(a, b, (((1,), (1,)), ((), ())), preferred_element_type=F32)


def _split3(x):
    hi = x.astype(BF16)
    r1 = x - hi.astype(F32)
    mid = r1.astype(BF16)
    lo = (r1 - mid.astype(F32)).astype(BF16)
    return hi, mid, lo


def _inproj_body(x_ref, g_ref, w_ref, uc_ref, q_ref, kv_ref, gt_ref):
    h = _rms(x_ref[...], g_ref[...]).astype(BF16)
    uc_ref[...] = _dot(h, w_ref[:, 0:COL_Q])
    q_ref[...] = (_dot(h, w_ref[:, COL_Q:COL_KV]) * (HEAD_DIM ** -0.5)).astype(BF16)
    kv_ref[...] = _dot(h, w_ref[:, COL_KV:COL_GATE]).astype(BF16)
    gt_ref[...] = _dot(h, w_ref[:, COL_GATE:N_IN_PAD])


def _inproj(x2d, g, w_pad, tm=256):
    T = x2d.shape[0]
    return pl.pallas_call(
        _inproj_body,
        grid=(T // tm,),
        in_specs=[pl.BlockSpec((tm, D_MODEL), lambda i: (i, 0)),
                  pl.BlockSpec((1, D_MODEL), lambda i: (0, 0)),
                  pl.BlockSpec((D_MODEL, N_IN_PAD), lambda i: (0, 0))],
        out_specs=[pl.BlockSpec((tm, COL_Q), lambda i: (i, 0)),
                   pl.BlockSpec((tm, COL_KV - COL_Q), lambda i: (i, 0)),
                   pl.BlockSpec((tm, COL_GATE - COL_KV), lambda i: (i, 0)),
                   pl.BlockSpec((tm, LANES), lambda i: (i, 0))],
        out_shape=[jax.ShapeDtypeStruct((T, COL_Q), F32),
                   jax.ShapeDtypeStruct((T, COL_KV - COL_Q), BF16),
                   jax.ShapeDtypeStruct((T, COL_GATE - COL_KV), BF16),
                   jax.ShapeDtypeStruct((T, LANES), F32)],
        compiler_params=_cparams(("parallel",)),
        name="inproj",
    )(x2d, g, w_pad)


CONV_HALO = 32


def _conv_body(ua_ref, ug_ref, ha_ref, hg_ref, w_ref, b_ref, lg_ref, lb_ref, o_ref, buf_ref, *, ts):
    i = pl.program_id(1)
    halo = ha_ref[...] * _sigmoid(hg_ref[...])
    buf_ref[0:CONV_HALO, :] = jnp.where(i > 0, halo, 0.0)
    buf_ref[CONV_HALO:CONV_HALO + ts, :] = ua_ref[...] * _sigmoid(ug_ref[...])
    off = CONV_HALO - (CONV_WIDTH - 1)
    acc = jnp.zeros((ts, CONV_CH), F32) + b_ref[...]
    for j in range(CONV_WIDTH):
        acc = acc + w_ref[j:j + 1, :] * buf_ref[off + j:off + j + ts, :]
    mu = jnp.mean(acc, axis=-1, keepdims=True)
    d = acc - mu
    var = jnp.mean(d * d, axis=-1, keepdims=True)
    yn = d * lax.rsqrt(var + NORM_EPS) * lg_ref[...] + lb_ref[...]
    o_ref[...] = (yn * _sigmoid(yn)).astype(o_ref.dtype)


def _conv(uc, B, S, w, b, lg, lb, ts=256):
    nT = S // ts
    r = ts // CONV_HALO
    cur = lambda col: (lambda bb, i: (bb * nT + i, col))
    prev = lambda col: (lambda bb, i: (jnp.maximum((bb * nT + i) * r - 1, 0), col))
    vec = lambda n: pl.BlockSpec((n, CONV_CH), lambda bb, i: (0, 0))
    return pl.pallas_call(
        functools.partial(_conv_body, ts=ts),
        grid=(B, nT),
        in_specs=[pl.BlockSpec((ts, CONV_CH), cur(0)), pl.BlockSpec((ts, CONV_CH), cur(1)),
                  pl.BlockSpec((CONV_HALO, CONV_CH), prev(0)), pl.BlockSpec((CONV_HALO, CONV_CH), prev(1)),
                  vec(CONV_WIDTH), vec(1), vec(1), vec(1)],
        out_specs=pl.BlockSpec((ts, CONV_CH), lambda bb, i: (bb * nT + i, 0)),
        out_shape=jax.ShapeDtypeStruct((B * S, CONV_CH), BF16),
        scratch_shapes=[pltpu.VMEM((CONV_HALO + ts, CONV_CH), F32)],
        compiler_params=_cparams(("parallel", "parallel")),
        name="conv",
    )(uc, uc, uc, uc, w, b, lg, lb)


N_BIAS = 6


def _pos_columns(pos_hi, pos_lo):
    n = pos_hi.shape[0]
    cols = np.zeros((n, LANES), np.float32)
    for half in (0, HEAD_DIM):
        for k in range(N_BIAS):
            cols[:, half + k] = pos_hi if k % 2 == 0 else pos_lo
    return cols


def _slope_columns():
    sl = jnp.asarray(2.0 ** (-8.0 * np.arange(1, N_HEADS + 1) / N_HEADS), F32)
    parts = _split3(sl)
    cols = jnp.zeros((N_HEADS, LANES), BF16)
    for half in (0, HEAD_DIM):
        for k in range(N_BIAS):
            cols = cols.at[:, half + k].set(parts[k // 2])
    cols = cols.reshape(N_KV_HEADS, GQA_GROUP, 1, LANES)
    return jnp.broadcast_to(cols, (N_KV_HEADS, GQA_GROUP, Q_BLOCK, LANES)).reshape(N_KV_HEADS, GQA_GROUP * Q_BLOCK, LANES)


def _compress_body(x_ref, pe_ref, w1_ref, b1_ref, w2_ref, b2_ref, c_ref, o_ref):
    half = (CMP_LEN // 2) * HEAD_DIM
    x = x_ref[...]
    first = _dot(x, w1_ref[0:half, :])
    second = _dot(x, w1_ref[half:2 * half, :])
    n = x.shape[0]
    second = pltpu.roll(second, n - 1, 0)
    pe8 = jnp.broadcast_to(pe_ref[...], (8, 2 * half)).astype(BF16)
    c1 = _dot(pe8, w1_ref[...])[0:1, :] + b1_ref[...]
    hid = first + second + c1
    act = (hid * _sigmoid(hid)).astype(BF16)
    o_ref[...] = (_dot(act, w2_ref[...]) + b2_ref[...] + c_ref[...]).astype(o_ref.dtype)


def _compress(kv16, pe, w1, b1, w2, b2):
    _, B, H, NC, W = kv16.shape
    c = np.arange(NC)
    consts = np.zeros((2, NC, LANES), np.float32)
    consts[0] = _pos_columns((c // 16) * 256.0, (c % 16) * 16.0)
    consts[0, :, :HEAD_DIM] = 0.0
    consts[1, :, HEAD_DIM:] = 1.0
    w2p = jnp.pad(w2, ((0, 0), (0, 0), (0, LANES - HEAD_DIM)))
    b2p = jnp.pad(b2, ((0, 0), (0, 0), (0, LANES - HEAD_DIM)))
    sel = lambda *rest: (lambda a, bb, h: (a,) + rest)
    return pl.pallas_call(
        _compress_body,
        grid=(2, B, H),
        in_specs=[pl.BlockSpec((None, None, None, NC, W), lambda a, bb, h: (a, bb, h, 0, 0)),
                  pl.BlockSpec((None, 1, W * 2), sel(0, 0)),
                  pl.BlockSpec((None, W * 2, CMP_HIDDEN), sel(0, 0)),
                  pl.BlockSpec((None, 1, CMP_HIDDEN), sel(0, 0)),
                  pl.BlockSpec((None, CMP_HIDDEN, LANES), sel(0, 0)),
                  pl.BlockSpec((None, 1, LANES), sel(0, 0)),
                  pl.BlockSpec((None, NC, LANES), sel(0, 0))],
        out_specs=pl.BlockSpec((None, None, None, NC, LANES), lambda a, bb, h: (a, bb, h, 0, 0)),
        out_shape=jax.ShapeDtypeStruct((2, B, H, NC, LANES), BF16),
        compiler_params=_cparams(("parallel", "parallel", "parallel")),
        name="compress",
    )(kv16, pe, w1, b1, w2p, b2p, jnp.asarray(consts))


NSA_TK = 512


def _nsa_body(q_ref, kc_ref, vc_ref, ks_ref, vs_ref, kw_ref, vw_ref, gate_ref, ovl_ref, hot_ref, pos_ref,
              slope_ref, o_ref, *, n_sel, n_cmp):
    par = pl.program_id(1) % 2
    i = pl.program_id(2)
    q0 = i * Q_BLOCK
    R = GQA_GROUP * Q_BLOCK
    lane = lax.broadcasted_iota(jnp.int32, (1, LANES), 1)
    low = lane < HEAD_DIM
    mine = (lane // HEAD_DIM) == par
    t = q0 + (lax.broadcasted_iota(jnp.int32, (R, 1), 0) & (Q_BLOCK - 1))

    qb = q_ref[...]
    in_low, in_mine = [], []
    for g in range(GQA_GROUP):
        col = qb[:, LANES * (g // 2):LANES * (g // 2 + 1)]
        swapped = pltpu.roll(col, HEAD_DIM, 1)
        in_low.append(col if g % 2 == 0 else swapped)
        in_mine.append(jnp.where((g % 2) == par, col, swapped))
    slope_cols = slope_ref[...]
    q_low = jnp.where(low, jnp.concatenate(in_low, axis=0), slope_cols)
    q_par = jnp.where(mine, jnp.concatenate(in_mine, axis=0), slope_cols)

    nc = kc_ref.shape[0]
    c_ix = lax.broadcasted_iota(jnp.int32, (1, nc), 1)
    mask_c = (t >= c_ix * CMP_STRIDE + (CMP_LEN - 1)) & (c_ix < n_cmp)
    s_c = jnp.where(mask_c, _dot_nt(q_low, kc_ref[...]), NEG)
    e_c = jnp.exp(s_c - jnp.max(s_c, axis=-1, keepdims=True))
    d_c = jnp.sum(e_c, axis=-1, keepdims=True)
    p_c = e_c * jnp.where(t >= CMP_LEN - 1, 1.0 / d_c, 0.0)
    o_c = _dot(p_c.astype(BF16), vc_ref[...])

    pc_sum = p_c[0:Q_BLOCK]
    for g in range(1, GQA_GROUP):
        pc_sum = pc_sum + p_c[g * Q_BLOCK:(g + 1) * Q_BLOCK]
    ovl = ovl_ref[...]
    imp = sum(_dot_nt(ovl, part) for part in _split3(pc_sum))
    j_ix = lax.broadcasted_iota(jnp.int32, (n_sel, 1), 0)
    j_f = j_ix.astype(F32)
    blk_t = (q0 + lax.broadcasted_iota(jnp.int32, (1, Q_BLOCK), 1)) // SEL_LEN
    forced = (j_ix == 0) | (j_ix == blk_t) | (j_ix == blk_t - 1)
    score = jnp.where((j_ix <= blk_t) & ~forced, imp, -FORCE_SCORE)
    sel = forced
    for _ in range(min(SEL_TOPK, n_sel) - N_FORCED):
        m = jnp.max(score, axis=0, keepdims=True)
        first = jnp.min(jnp.where(score == m, j_f, float(n_sel)), axis=0, keepdims=True)
        hit = j_f == first
        sel = sel | (hit & (m > -0.5 * FORCE_SCORE))
        score = jnp.where(hit, TAKEN, score)
    sel_bias = jnp.where(sel, 0.0, NEG).T.astype(BF16)
    q_sel = jnp.concatenate([q_par, jnp.concatenate([sel_bias] * GQA_GROUP, axis=0)], axis=1)

    def k_tile(k_ref, start, size):
        return jnp.where(mine, k_ref[pl.ds(start, size), :], pos_ref[pl.ds(start, size), :])

    def v_tile(v_ref, start, size):
        return jnp.where(mine, v_ref[pl.ds(start, size), :], jnp.ones((), BF16))

    span = WIN + Q_BLOCK
    w0 = pl.multiple_of(jnp.maximum(q0 - WIN, 0), Q_BLOCK)
    dist_w = t - (w0 + lax.broadcasted_iota(jnp.int32, (1, span), 1))
    mask_w = dist_w.astype(jnp.uint32) < WIN
    s_w = jnp.where(mask_w, _dot_nt(q_par, k_tile(kw_ref, w0, span)), NEG)
    e_w = jnp.exp(s_w - jnp.max(s_w, axis=-1, keepdims=True))
    acc_w = _dot(e_w.astype(BF16), v_tile(vw_ref, w0, span))

    def scores(k0):
        k_aug = jnp.concatenate([k_tile(ks_ref, k0, NSA_TK), hot_ref[pl.ds(k0, NSA_TK), :]], axis=1)
        return _dot_nt(q_sel, k_aug)

    def absorb(s, k0, m_run, acc):
        m_new = jnp.maximum(m_run, jnp.max(s, axis=-1, keepdims=True))
        p = jnp.exp(s - m_new)
        acc = jnp.exp(m_run - m_new) * acc + _dot(p.astype(BF16), v_tile(vs_ref, k0, NSA_TK))
        return m_new, acc

    def sweep(kt, carry):
        m_run, acc, s = carry
        k0 = pl.multiple_of(kt * NSA_TK, NSA_TK)
        s_next = scores(k0 + NSA_TK)
        m_run, acc = absorb(s, k0, m_run, acc)
        return m_run, acc, s_next

    n_full = q0 // NSA_TK
    carry = (jnp.full((R, 1), NEG, F32), jnp.zeros((R, LANES), F32), scores(0))
    m_run, acc_s, s_last = lax.fori_loop(0, n_full, sweep, carry)
    k_last = pl.multiple_of(n_full * NSA_TK, NSA_TK)
    s_last = jnp.where(t >= k_last + lax.broadcasted_iota(jnp.int32, (1, NSA_TK), 1), s_last, NEG)
    _, acc_s = absorb(s_last, k_last, m_run, acc_s)

    def normalised(acc):
        return acc / jnp.where(mine, pltpu.roll(acc, HEAD_DIM, 1), 1.0)

    o_sw_all = (normalised(acc_s), normalised(acc_w))
    gate = _sigmoid(gate_ref[...])
    placed = []
    for g in range(GQA_GROUP):
        rows = slice(g * Q_BLOCK, (g + 1) * Q_BLOCK)
        o_sw = (gate[:, GQA_GROUP + g:GQA_GROUP + g + 1] * o_sw_all[0][rows]
                + gate[:, 2 * GQA_GROUP + g:2 * GQA_GROUP + g + 1] * o_sw_all[1][rows])
        o_cg = gate[:, g:g + 1] * o_c[rows]
        if g % 2 == 1:
            o_cg = pltpu.roll(o_cg, HEAD_DIM, 1)
        o_sw = jnp.where((g % 2) == par, o_sw, pltpu.roll(o_sw, HEAD_DIM, 1))
        placed.append(o_cg + o_sw)
    out = [jnp.where(low, placed[2 * c], placed[2 * c + 1]) for c in range(GQA_GROUP // 2)]
    o_ref[...] = jnp.concatenate(out, axis=1).astype(o_ref.dtype)


def _selection_overlap_t(n_cmp_pad, n_cmp, n_sel):
    cs = np.arange(n_cmp_pad) * CMP_STRIDE
    ce = cs + CMP_LEN - 1
    js = np.arange(n_sel) * SEL_LEN
    m = (cs[None, :] <= js[:, None] + SEL_LEN - 1) & (ce[None, :] >= js[:, None])
    m &= (np.arange(n_cmp_pad) < n_cmp)[None, :]
    return m.astype(np.float32)


def _nsa(q, cmp_kv, kv, gates, B, S):
    n_sel = S // SEL_LEN
    assert n_sel >= SEL_TOPK and NSA_TK % Q_BLOCK == 0 and S % NSA_TK == 0 and S >= WIN + Q_BLOCK
    n_cmp = (S - CMP_LEN) // CMP_STRIDE + 1
    nc = S // CMP_STRIDE
    nQ = S // Q_BLOCK
    T = B * S
    ovl = jnp.asarray(_selection_overlap_t(nc, n_cmp, n_sel), BF16)
    pos = np.arange(S)
    hot = jnp.asarray((pos[:, None] // SEL_LEN) == np.arange(n_sel)[None, :], BF16)
    pos_cols = jnp.asarray(_pos_columns((pos // 64) * 64.0, (pos % 64) * 1.0), BF16)
    cmp_spec = lambda a: pl.BlockSpec((None, None, None, nc, LANES), lambda bb, h, i: (a, bb, h, 0, 0))
    kv_spec = lambda a: pl.BlockSpec((S, LANES), lambda bb, h, i: (bb, 2 * a + h // 2))
    const = lambda shape: pl.BlockSpec(shape, lambda bb, h, i: (0,) * len(shape))
    return pl.pallas_call(
        functools.partial(_nsa_body, n_sel=n_sel, n_cmp=n_cmp),
        grid=(B, N_KV_HEADS, nQ),
        in_specs=[pl.BlockSpec((Q_BLOCK, GQA_GROUP * HEAD_DIM), lambda bb, h, i: (bb * nQ + i, h)),
                  cmp_spec(0), cmp_spec(1), kv_spec(2), kv_spec(3), kv_spec(4), kv_spec(5),
                  pl.BlockSpec((None, None, Q_BLOCK, 3 * GQA_GROUP), lambda bb, h, i: (bb, h, i, 0)),
                  const((n_sel, nc)), const((S, n_sel)), const((S, LANES)),
                  pl.BlockSpec((None, GQA_GROUP * Q_BLOCK, LANES), lambda bb, h, i: (h, 0, 0))],
        out_specs=pl.BlockSpec((Q_BLOCK, GQA_GROUP * HEAD_DIM), lambda bb, h, i: (bb * nQ + i, h)),
        out_shape=jax.ShapeDtypeStruct((T, N_HEADS * HEAD_DIM), BF16),
        compiler_params=_cparams(("parallel", "parallel", "arbitrary")),
        name="nsa",
    )(q, cmp_kv, cmp_kv, kv, kv, kv, kv, gates, ovl, hot, pos_cols, _slope_columns())


def _memkv_body(m_ref, g_ref, wk_ref, wv_ref, k_ref, v_ref):
    h = _rms(m_ref[...], g_ref[...]).astype(BF16)
    k_ref[...] = _dot(h, wk_ref[...]).astype(BF16)
    v_ref[...] = _dot(h, wv_ref[...]).astype(BF16)


def _memkv(mem2d, g, wk, wv, tm=256):
    R = mem2d.shape[0]
    wspec = pl.BlockSpec((D_MODEL, XW), lambda i: (0, 0))
    ospec = pl.BlockSpec((tm, XW), lambda i: (i, 0))
    return pl.pallas_call(
        _memkv_body,
        grid=(R // tm,),
        in_specs=[pl.BlockSpec((tm, D_MODEL), lambda i: (i, 0)), pl.BlockSpec((1, D_MODEL), lambda i: (0, 0)),
                  wspec, wspec],
        out_specs=[ospec, ospec],
        out_shape=[jax.ShapeDtypeStruct((R, XW), BF16)] * 2,
        compiler_params=_cparams(("parallel",)),
        name="memkv",
    )(mem2d, g, wk, wv)


def _mid_body(x_ref, yc_ref, yn_ref, wo_ref, gx_ref, wq_ref, mk_ref, mv_ref, wxo_ref, gf_ref, wr_ref, br_ref,
              x2_ref, h3_ref, idx_ref, gate_ref):
    x1 = x_ref[...] + _dot(yc_ref[...], wo_ref[0:CONV_CH, :]) + _dot(yn_ref[...], wo_ref[CONV_CH:2 * CONV_CH, :])
    h = _rms(x1, gx_ref[...]).astype(BF16)
    q = (_dot(h, wq_ref[...]) * (XHEAD_DIM ** -0.5)).astype(BF16)
    heads = []
    for hd in range(N_XHEADS):
        cols = slice(hd * XHEAD_DIM, (hd + 1) * XHEAD_DIM)
        s = _dot_nt(q[:, cols], mk_ref[:, cols])
        e = jnp.exp(s - jnp.max(s, axis=-1, keepdims=True))
        o = _dot(e.astype(BF16), mv_ref[:, cols]) / jnp.sum(e, axis=-1, keepdims=True)
        heads.append(o.astype(BF16))
    x2 = x1 + _dot(jnp.concatenate(heads, axis=-1), wxo_ref[...])
    x2_ref[...] = x2
    h3 = _rms(x2, gf_ref[...])
    h3_ref[...] = h3
    h_hi, h_mid, _ = _split3(h3)
    w_hi, w_mid, _ = _split3(wr_ref[...])
    logits = _dot(h_hi, w_hi) + _dot(h_hi, w_mid) + _dot(h_mid, w_hi) + br_ref[...]
    lane = lax.broadcasted_iota(jnp.int32, (1, LANES), 1)
    lane_f = lane.astype(F32)
    idx_out = jnp.zeros(logits.shape, F32)
    val_out = jnp.full(logits.shape, NEG, F32)
    for k in range(TOP_K):
        m = jnp.max(logits, axis=-1, keepdims=True)
        first = jnp.min(jnp.where(logits == m, lane_f, float(LANES)), axis=-1, keepdims=True)
        idx_out = jnp.where(lane == k, first, idx_out)
        val_out = jnp.where(lane == k, m, val_out)
        logits = jnp.where(lane_f == first, TAKEN, logits)
    e = jnp.where(lane < TOP_K, jnp.exp(val_out - jnp.max(val_out, axis=-1, keepdims=True)), 0.0)
    idx_ref[...] = idx_out.astype(jnp.int32)
    gate_ref[...] = e / jnp.sum(e, axis=-1, keepdims=True)


def _mid(x2d, yc, yn, w_out, g_x, w_xq, mk, mv, w_xo, g_ffn, w_r, b_r, B, S, tm=256):
    T = x2d.shape[0]
    nT = S // tm
    M = mk.shape[0] // B
    row = lambda w: pl.BlockSpec((tm, w), lambda i: (i, 0))
    full = lambda a: pl.BlockSpec(a.shape, lambda i: (0,) * a.ndim)
    memspec = pl.BlockSpec((M, XW), lambda i: (i // nT, 0))
    return pl.pallas_call(
        _mid_body,
        grid=(T // tm,),
        in_specs=[row(D_MODEL), row(CONV_CH), row(CONV_CH), full(w_out), full(g_x), full(w_xq), memspec, memspec,
                  full(w_xo), full(g_ffn), full(w_r), full(b_r)],
        out_specs=[row(D_MODEL), row(D_MODEL), row(LANES), row(LANES)],
        out_shape=[jax.ShapeDtypeStruct((T, D_MODEL), F32), jax.ShapeDtypeStruct((T, D_MODEL), F32),
                   jax.ShapeDtypeStruct((T, LANES), jnp.int32), jax.ShapeDtypeStruct((T, LANES), F32)],
        compiler_params=_cparams(("parallel",)),
        name="mid",
    )(x2d, yc, yn, w_out, g_x, w_xq, mk, mv, w_xo, g_ffn, w_r, b_r)


MOE_TM = 512
MOE_TF = 256
MOE_ST = 4


def _gmm_body(se_ref, st0_ref, nrt_ref, sb_ref, nsu_ref, tok_ref, h_hbm, wg_ref, wl_ref, bg_ref, bl_ref, wd_ref,
              bd_ref, y_hbm, stage, hb16, acc_ref, sem_in, sem_out):
    s = pl.program_id(0)
    f = pl.program_id(1)
    nf = pl.num_programs(1)
    n_assign = tok_ref.shape[0]

    def out_copy(src, tile):
        row0 = pl.multiple_of(tile * MOE_TM, MOE_TM)
        return pltpu.make_async_copy(src, y_hbm.at[pl.ds(row0, MOE_TM), :], sem_out)

    def for_tiles(sup_ix, fn):
        def body(r, c):
            fn(r, st0_ref[sup_ix] + r)
            return c
        lax.fori_loop(0, nrt_ref[sup_ix], body, 0)

    @pl.when(s < nsu_ref[0])
    def _():
        n_rt = nrt_ref[s]

        @pl.when(f == 0)
        def _():
            def gather_tile(r, c):
                base = sb_ref[s] + r * MOE_TM

                def issue(j, c2):
                    tok = tok_ref[jnp.minimum(base + j, n_assign - 1)]
                    pltpu.make_async_copy(h_hbm.at[pl.ds(tok, 1), :], stage.at[pl.ds(j, 1), :], sem_in).start()
                    return c2
                lax.fori_loop(0, MOE_TM, issue, 0, unroll=8)
                pltpu.make_async_copy(h_hbm.at[pl.ds(0, MOE_TM), :], stage, sem_in).wait()
                hb16[r] = stage[...].astype(BF16)
                return c
            lax.fori_loop(0, n_rt, gather_tile, 0)

            @pl.when(s > 0)
            def _():
                for_tiles(s - 1, lambda r, tile: out_copy(acc_ref.at[r], tile).wait())

            def start_from_bias(r, tile):
                acc_ref[r] = jnp.broadcast_to(bd_ref[...], (MOE_TM, D_MODEL))
            for_tiles(s, start_from_bias)

        wg = wg_ref[...].astype(BF16)
        wl = wl_ref[...].astype(BF16)
        wd = wd_ref[...].astype(BF16)

        def row_tile(r, c):
            h = hb16[r]
            x_glu = jnp.minimum(_dot(h, wg) + bg_ref[...], SWIGLU_LIMIT)
            x_lin = jnp.clip(_dot(h, wl) + bl_ref[...], -SWIGLU_LIMIT, SWIGLU_LIMIT)
            act = x_glu * _sigmoid(SWIGLU_ALPHA * x_glu) * (x_lin + 1.0)
            acc_ref[r] += _dot(act.astype(BF16), wd)
            return c
        lax.fori_loop(0, n_rt, row_tile, 0)

        @pl.when(f == nf - 1)
        def _():
            for_tiles(s, lambda r, tile: out_copy(acc_ref.at[r], tile).start())

            @pl.when(s == nsu_ref[0] - 1)
            def _():
                for_tiles(s, lambda r, tile: out_copy(acc_ref.at[r], tile).wait())
                stage[...] = jnp.zeros(stage.shape, F32)

                def fill(tile, c):
                    cp = out_copy(stage, tile)
                    cp.start()
                    cp.wait()
                    return c
                lax.fori_loop(st0_ref[s] + n_rt, y_hbm.shape[0] // MOE_TM, fill, 0)


def _gmm(sup, sorted_tok, h3, w_up, b_up, w_down, b_down, n_rows):
    se, st0, nrt, sb, nsu = sup
    nf = D_FF // MOE_TF

    def live(s, f, se_, nsu_):
        ok = s < nsu_[0]
        return jnp.where(ok, se_[s], se_[jnp.maximum(nsu_[0] - 1, 0)]), jnp.where(ok, f, nf - 1)

    def wg_map(s, f, se_, st0_, nrt_, sb_, nsu_, tok_):
        e, ff = live(s, f, se_, nsu_)
        return (e, 0, ff)

    def wl_map(s, f, se_, st0_, nrt_, sb_, nsu_, tok_):
        e, ff = live(s, f, se_, nsu_)
        return (e, 0, nf + ff)

    def wd_map(s, f, se_, st0_, nrt_, sb_, nsu_, tok_):
        e, ff = live(s, f, se_, nsu_)
        return (e, ff, 0)

    def bd_map(s, f, se_, st0_, nrt_, sb_, nsu_, tok_):
        e, _ = live(s, f, se_, nsu_)
        return (e, 0, 0)

    grid_spec = pltpu.PrefetchScalarGridSpec(
        num_scalar_prefetch=6,
        grid=(se.shape[0], nf),
        in_specs=[pl.BlockSpec(memory_space=pl.ANY),
                  pl.BlockSpec((None, D_MODEL, MOE_TF), wg_map),
                  pl.BlockSpec((None, D_MODEL, MOE_TF), wl_map),
                  pl.BlockSpec((None, 1, MOE_TF), wg_map),
                  pl.BlockSpec((None, 1, MOE_TF), wl_map),
                  pl.BlockSpec((None, MOE_TF, D_MODEL), wd_map),
                  pl.BlockSpec((None, 1, D_MODEL), bd_map)],
        out_specs=pl.BlockSpec(memory_space=pl.ANY),
        scratch_shapes=[pltpu.VMEM((MOE_TM, D_MODEL), F32), pltpu.VMEM((MOE_ST, MOE_TM, D_MODEL), BF16),
                        pltpu.VMEM((MOE_ST, MOE_TM, D_MODEL), F32), pltpu.SemaphoreType.DMA(()),
                        pltpu.SemaphoreType.DMA(())],
    )
    return pl.pallas_call(
        _gmm_body,
        grid_spec=grid_spec,
        out_shape=jax.ShapeDtypeStruct((n_rows, D_MODEL), F32),
        compiler_params=_cparams(("arbitrary", "arbitrary")),
        name="moe_gmm",
    )(se, st0, nrt, sb, nsu, sorted_tok, h3, w_up, w_up, b_up, b_up, w_down, b_down)


CMB_TT = 128


def _combine_body(pos_ref, y_hbm, gate_ref, x_ref, g_ref, o_ref, buf, sem):
    i = pl.program_id(0)

    def issue(r, c):
        for k in range(TOP_K):
            p = pos_ref[(i * CMB_TT + r) * TOP_K + k]
            pltpu.make_async_copy(y_hbm.at[pl.ds(p, 1), :], buf.at[k, pl.ds(r, 1), :], sem).start()
        return c
    lax.fori_loop(0, CMB_TT, issue, 0, unroll=4)
    for k in range(TOP_K):
        pltpu.make_async_copy(y_hbm.at[pl.ds(0, CMB_TT), :], buf.at[k], sem).wait()
    gate = gate_ref[...]
    y = x_ref[...]
    for k in range(TOP_K):
        y = y + gate[:, k:k + 1] * buf[k]
    o_ref[...] = _rms(y, g_ref[...])


def _combine(pos, y_rows, gate, x2, g_final):
    T = x2.shape[0]
    grid_spec = pltpu.PrefetchScalarGridSpec(
        num_scalar_prefetch=1,
        grid=(T // CMB_TT,),
        in_specs=[pl.BlockSpec(memory_space=pl.ANY),
                  pl.BlockSpec((CMB_TT, LANES), lambda i, p: (i, 0)),
                  pl.BlockSpec((CMB_TT, D_MODEL), lambda i, p: (i, 0)),
                  pl.BlockSpec((1, D_MODEL), lambda i, p: (0, 0))],
        out_specs=pl.BlockSpec((CMB_TT, D_MODEL), lambda i, p: (i, 0)),
        scratch_shapes=[pltpu.VMEM((TOP_K, CMB_TT, D_MODEL), F32), pltpu.SemaphoreType.DMA(())],
    )
    return pl.pallas_call(
        _combine_body,
        grid_spec=grid_spec,
        out_shape=jax.ShapeDtypeStruct((T, D_MODEL), F32),
        compiler_params=_cparams(("arbitrary",)),
        name="moe_combine",
    )(pos, y_rows, gate, x2, g_final)


def _route(top_idx):
    T = top_idx.shape[0]
    n_assign = T * TOP_K
    flat_e = top_idx.reshape(-1)
    order = jnp.argsort(flat_e, stable=True)
    onehot = (flat_e[:, None] == jnp.arange(N_EXPERTS, dtype=jnp.int32)[None, :]).astype(jnp.int32)
    running = jnp.cumsum(onehot, axis=0)
    counts = running[-1]
    within = jnp.sum(onehot * (running - 1), axis=1)
    padded = (counts + MOE_TM - 1) // MOE_TM * MOE_TM
    start = jnp.cumsum(counts) - counts
    pend = jnp.cumsum(padded)
    pstart = pend - padded
    n_tiles = -(-n_assign // MOE_TM) + N_EXPERTS
    tiles_e = padded // MOE_TM
    supers_e = (tiles_e + MOE_ST - 1) // MOE_ST
    s_end = jnp.cumsum(supers_e)
    s_ix = jnp.arange(-(-n_tiles // MOE_ST) + N_EXPERTS, dtype=jnp.int32)
    s_e = jnp.minimum(jnp.sum(s_ix[:, None] >= s_end[None, :], axis=1), N_EXPERTS - 1).astype(jnp.int32)
    local = s_ix - (s_end - supers_e)[s_e]
    i32 = lambda v: v.astype(jnp.int32)
    sup = (s_e, i32(pstart[s_e] // MOE_TM + MOE_ST * local), i32(jnp.clip(tiles_e[s_e] - MOE_ST * local, 0, MOE_ST)),
           i32(start[s_e] + MOE_ST * MOE_TM * local), i32(s_end[-1]).reshape(1))
    sorted_tok = (order // TOP_K).astype(jnp.int32)
    pos = (pstart[flat_e] + within).astype(jnp.int32)
    return sup, sorted_tok, pos, n_tiles * MOE_TM


def _layer(x, mem, g_mix, w_in, conv_w, conv_b, conv_ln_g, conv_ln_b, pe_cmp, w_cmp1, b_cmp1, w_cmp2, b_cmp2,
           w_out, g_x, g_mem, w_xq, w_xk, w_xv, w_xo, g_ffn, w_router, b_router, w_up, b_up, w_down, b_down):
    B, S, D = x.shape
    T = B * S
    x2d = x.reshape(T, D)
    row = lambda v: v.reshape(1, -1)

    w_pad = jnp.pad(w_in, ((0, 0), (0, N_IN_PAD - w_in.shape[1]))).astype(BF16)
    uc, q, kv, gt = _inproj(x2d, row(g_mix), w_pad)

    y_conv = _conv(uc, B, S, conv_w, row(conv_b), row(conv_ln_g), row(conv_ln_b))

    kv16 = kv[:, :2 * KV_WIDTH].reshape(B, S, 2, N_KV_HEADS, HEAD_DIM).transpose(2, 0, 3, 1, 4) \
        .reshape(2, B, N_KV_HEADS, S // CMP_STRIDE, CMP_STRIDE * HEAD_DIM)
    cmp_kv = _compress(kv16, pe_cmp.reshape(2, 1, CMP_LEN * HEAD_DIM), w_cmp1.astype(BF16),
                       b_cmp1.reshape(2, 1, CMP_HIDDEN), w_cmp2.astype(BF16), b_cmp2.reshape(2, 1, HEAD_DIM))
    gates = gt[:, :N_GATE].reshape(B, S, 3, N_KV_HEADS, GQA_GROUP).transpose(0, 3, 1, 2, 4) \
        .reshape(B, N_KV_HEADS, S, 3 * GQA_GROUP)
    y_nsa = _nsa(q, cmp_kv, kv, gates, B, S)

    M = mem.shape[1]
    mk, mv = _memkv(mem.reshape(B * M, D), row(g_mem), w_xk.astype(BF16), w_xv.astype(BF16))
    w_r = jnp.pad(w_router, ((0, 0), (0, LANES - N_EXPERTS)))
    b_r = jnp.pad(b_router, (0, LANES - N_EXPERTS), constant_values=NEG).reshape(1, LANES)
    x2, h3, idx, gate = _mid(x2d, y_conv, y_nsa, w_out.astype(BF16), row(g_x), w_xq.astype(BF16), mk, mv,
                             w_xo.astype(BF16), row(g_ffn), w_r, b_r, B, S)

    sup, sorted_tok, pos, n_rows = _route(idx[:, :TOP_K])
    y_rows = _gmm(sup, sorted_tok, h3, w_up, b_up.reshape(N_EXPERTS, 1, 2 * D_FF), w_down,
                  b_down.reshape(N_EXPERTS, 1, D_MODEL), n_rows)
    return x2, y_rows, pos, gate


def kernel(x, mem, g_mix, w_in, conv_w, conv_b, conv_ln_g, conv_ln_b, pe_cmp, w_cmp1, b_cmp1, w_cmp2, b_cmp2, w_out, g_x, g_mem, w_xq, w_xk, w_xv, w_xo, g_ffn, w_router, b_router, w_up, b_up, w_down, b_down, g_final):
    B, S, D = x.shape
    assert g_mix.shape[0] == 1, "single-layer block"
    x2, y_rows, pos, gate = _layer(x, mem, g_mix[0], w_in[0], conv_w[0], conv_b[0], conv_ln_g[0], conv_ln_b[0],
                                   pe_cmp[0], w_cmp1[0], b_cmp1[0], w_cmp2[0], b_cmp2[0], w_out[0], g_x[0],
                                   g_mem[0], w_xq[0], w_xk[0], w_xv[0], w_xo[0], g_ffn[0], w_router[0],
                                   b_router[0], w_up[0], b_up[0], w_down[0], b_down[0])
    out = _combine(pos, y_rows, gate, x2, g_final.reshape(1, D))
    return out.reshape(B, S, D)
```

```python
import functools

import numpy as np
import jax
import jax.numpy as jnp
from jax import lax
from jax.experimental import pallas as pl
from jax.experimental.pallas import tpu as pltpu

F32 = jnp.float32
BF16 = jnp.bfloat16

D_MODEL = 2048
CONV_CH = 1024
CONV_WIDTH = 31
N_HEADS = 16
HEAD_DIM = 64
N_KV_HEADS = 4
GQA_GROUP = 4
KV_WIDTH = N_KV_HEADS * HEAD_DIM
CMP_LEN = 32
CMP_STRIDE = 16
CMP_HIDDEN = 256
SEL_LEN = 64
SEL_TOPK = 16
N_FORCED = 3
WIN = 512
Q_BLOCK = 256
FORCE_SCORE = 1.0e4
N_XHEADS = 4
XHEAD_DIM = 128
XW = N_XHEADS * XHEAD_DIM
N_EXPERTS = 32
TOP_K = 4
D_FF = D_MODEL
SWIGLU_LIMIT = 7.0
SWIGLU_ALPHA = 1.702
NORM_EPS = 1e-5
NEG = -1e30
TAKEN = -3e38
LANES = 128
SUBLANES = 8
N_GATE = 3 * N_HEADS
COL_Q = 2 * CONV_CH
COL_KV = COL_Q + N_HEADS * HEAD_DIM
COL_GATE = COL_KV + 6 * KV_WIDTH
N_IN_PAD = COL_GATE + LANES
VMEM_LIMIT = 56 * 1024 * 1024


def _cparams(sem):
    return pltpu.CompilerParams(dimension_semantics=sem, vmem_limit_bytes=VMEM_LIMIT)


def _rms(x, g):
    return x * lax.rsqrt(jnp.mean(x * x, axis=-1, keepdims=True) + NORM_EPS) * g


def _sigmoid(x):
    return 1.0 / (1.0 + jnp.exp(-x))


def _dot(a, b):
    return jnp.dot(a, b, preferred_element_type=F32)


def _dot_nt(a, b):
    return lax.dot_general(a, b, (((1,), (1,)), ((), ())), preferred_element_type=F32)


def _split3(x):
    hi = x.astype(BF16)
    r1 = x - hi.astype(F32)
    mid = r1.astype(BF16)
    lo = (r1 - mid.astype(F32)).astype(BF16)
    return hi, mid, lo


def _inproj_body(x_ref, g_ref, w_ref, uc_ref, q_ref, kv_ref, gt_ref):
    h = _rms(x_ref[...], g_ref[...]).astype(BF16)
    uc_ref[...] = _dot(h, w_ref[:, 0:COL_Q])
    q_ref[...] = (_dot(h, w_ref[:, COL_Q:COL_KV]) * (HEAD_DIM ** -0.5)).astype(BF16)
    kv_ref[...] = _dot(h, w_ref[:, COL_KV:COL_GATE]).astype(BF16)
    gt_ref[...] = _dot(h, w_ref[:, COL_GATE:N_IN_PAD])


def _inproj(x2d, g, w_pad, tm=256):
    T = x2d.shape[0]
    return pl.pallas_call(
        _inproj_body,
        grid=(T // tm,),
        in_specs=[pl.BlockSpec((tm, D_MODEL), lambda i: (i, 0)),
                  pl.BlockSpec((1, D_MODEL), lambda i: (0, 0)),
                  pl.BlockSpec((D_MODEL, N_IN_PAD), lambda i: (0, 0))],
        out_specs=[pl.BlockSpec((tm, COL_Q), lambda i: (i, 0)),
                   pl.BlockSpec((tm, COL_KV - COL_Q), lambda i: (i, 0)),
                   pl.BlockSpec((tm, COL_GATE - COL_KV), lambda i: (i, 0)),
                   pl.BlockSpec((tm, LANES), lambda i: (i, 0))],
        out_shape=[jax.ShapeDtypeStruct((T, COL_Q), F32),
                   jax.ShapeDtypeStruct((T, COL_KV - COL_Q), BF16),
                   jax.ShapeDtypeStruct((T, COL_GATE - COL_KV), BF16),
                   jax.ShapeDtypeStruct((T, LANES), F32)],
        compiler_params=_cparams(("parallel",)),
        name="inproj",
    )(x2d, g, w_pad)


CONV_HALO = 32


def _conv_body(ua_ref, ug_ref, ha_ref, hg_ref, w_ref, b_ref, lg_ref, lb_ref, o_ref, buf_ref, sh_ref, *, ts):
    i = pl.program_id(1)
    halo = ha_ref[...] * _sigmoid(hg_ref[...])
    buf_ref[0:CONV_HALO, :] = jnp.where(i > 0, halo, 0.0)
    buf_ref[CONV_HALO:CONV_HALO + ts, :] = ua_ref[...] * _sigmoid(ug_ref[...])
    off = CONV_HALO - (CONV_WIDTH - 1)
    acc = jnp.zeros((ts, CONV_CH), F32) + b_ref[...]
    for phase in range(SUBLANES):
        taps = [j for j in range(CONV_WIDTH) if (off + j) % SUBLANES == phase]
        first = off + taps[0]
        length = taps[-1] - taps[0] + ts
        sh_ref[0:length, :] = buf_ref[first:first + length, :]
        for j in taps:
            lo = j - taps[0]
            acc = acc + w_ref[j:j + 1, :] * sh_ref[lo:lo + ts, :]
    mu = jnp.mean(acc, axis=-1, keepdims=True)
    d = acc - mu
    var = jnp.mean(d * d, axis=-1, keepdims=True)
    yn = d * lax.rsqrt(var + NORM_EPS) * lg_ref[...] + lb_ref[...]
    o_ref[...] = (yn * _sigmoid(yn)).astype(o_ref.dtype)


def _conv(uc, B, S, w, b, lg, lb, ts=256):
    nT = S // ts
    r = ts // CONV_HALO
    cur = lambda col: (lambda bb, i: (bb * nT + i, col))
    prev = lambda col: (lambda bb, i: (jnp.maximum((bb * nT + i) * r - 1, 0), col))
    vec = lambda n: pl.BlockSpec((n, CONV_CH), lambda bb, i: (0, 0))
    return pl.pallas_call(
        functools.partial(_conv_body, ts=ts),
        grid=(B, nT),
        in_specs=[pl.BlockSpec((ts, CONV_CH), cur(0)), pl.BlockSpec((ts, CONV_CH), cur(1)),
                  pl.BlockSpec((CONV_HALO, CONV_CH), prev(0)), pl.BlockSpec((CONV_HALO, CONV_CH), prev(1)),
                  vec(CONV_WIDTH), vec(1), vec(1), vec(1)],
        out_specs=pl.BlockSpec((ts, CONV_CH), lambda bb, i: (bb * nT + i, 0)),
        out_shape=jax.ShapeDtypeStruct((B * S, CONV_CH), BF16),
        scratch_shapes=[pltpu.VMEM((CONV_HALO + ts, CONV_CH), F32), pltpu.VMEM((CONV_HALO + ts, CONV_CH), F32)],
        compiler_params=_cparams(("parallel", "parallel")),
        name="conv",
    )(uc, uc, uc, uc, w, b, lg, lb)


N_BIAS = 6


def _pos_columns(pos_hi, pos_lo):
    n = pos_hi.shape[0]
    cols = np.zeros((n, LANES), np.float32)
    for half in (0, HEAD_DIM):
        for k in range(N_BIAS):
            cols[:, half + k] = pos_hi if k % 2 == 0 else pos_lo
    return cols


def _slope_columns():
    sl = jnp.asarray(2.0 ** (-8.0 * np.arange(1, N_HEADS + 1) / N_HEADS), F32)
    parts = _split3(sl)
    cols = jnp.zeros((N_HEADS, LANES), BF16)
    for half in (0, HEAD_DIM):
        for k in range(N_BIAS):
            cols = cols.at[:, half + k].set(parts[k // 2])
    cols = cols.reshape(N_KV_HEADS, GQA_GROUP, 1, LANES)
    return jnp.broadcast_to(cols, (N_KV_HEADS, GQA_GROUP, Q_BLOCK, LANES)).reshape(N_KV_HEADS, GQA_GROUP * Q_BLOCK, LANES)


def _compress_body(x_ref, pe_ref, w1_ref, b1_ref, w2_ref, b2_ref, c_ref, o_ref):
    half = (CMP_LEN // 2) * HEAD_DIM
    x = x_ref[...]
    first = _dot(x, w1_ref[0:half, :])
    second = _dot(x, w1_ref[half:2 * half, :])
    n = x.shape[0]
    second = pltpu.roll(second, n - 1, 0)
    pe8 = jnp.broadcast_to(pe_ref[...], (8, 2 * half)).astype(BF16)
    c1 = _dot(pe8, w1_ref[...])[0:1, :] + b1_ref[...]
    hid = first + second + c1
    act = (hid * _sigmoid(hid)).astype(BF16)
    o_ref[...] = (_dot(act, w2_ref[...]) + b2_ref[...] + c_ref[...]).astype(o_ref.dtype)


def _compress(kv16, pe, w1, b1, w2, b2):
    _, B, H, NC, W = kv16.shape
    c = np.arange(NC)
    consts = np.zeros((2, NC, LANES), np.float32)
    consts[0] = _pos_columns((c // 16) * 256.0, (c % 16) * 16.0)
    consts[0, :, :HEAD_DIM] = 0.0
    consts[1, :, HEAD_DIM:] = 1.0
    w2p = jnp.pad(w2, ((0, 0), (0, 0), (0, LANES - HEAD_DIM)))
    b2p = jnp.pad(b2, ((0, 0), (0, 0), (0, LANES - HEAD_DIM)))
    sel = lambda *rest: (lambda a, bb, h: (a,) + rest)
    return pl.pallas_call(
        _compress_body,
        grid=(2, B, H),
        in_specs=[pl.BlockSpec((None, None, None, NC, W), lambda a, bb, h: (a, bb, h, 0, 0)),
                  pl.BlockSpec((None, 1, W * 2), sel(0, 0)),
                  pl.BlockSpec((None, W * 2, CMP_HIDDEN), sel(0, 0)),
                  pl.BlockSpec((None, 1, CMP_HIDDEN), sel(0, 0)),
                  pl.BlockSpec((None, CMP_HIDDEN, LANES), sel(0, 0)),
                  pl.BlockSpec((None, 1, LANES), sel(0, 0)),
                  pl.BlockSpec((None, NC, LANES), sel(0, 0))],
        out_specs=pl.BlockSpec((None, None, None, NC, LANES), lambda a, bb, h: (a, bb, h, 0, 0)),
        out_shape=jax.ShapeDtypeStruct((2, B, H, NC, LANES), BF16),
        compiler_params=_cparams(("parallel", "parallel", "parallel")),
        name="compress",
    )(kv16, pe, w1, b1, w2p, b2p, jnp.asarray(consts))


NSA_TK = 512


def _nsa_body(q_ref, kc_ref, vc_ref, ks_ref, vs_ref, kw_ref, vw_ref, gate_ref, ovl_ref, hot_ref, pos_ref,
              slope_ref, o_ref, *, n_sel, n_cmp):
    par = pl.program_id(1) % 2
    i = pl.program_id(2)
    q0 = i * Q_BLOCK
    R = GQA_GROUP * Q_BLOCK
    lane = lax.broadcasted_iota(jnp.int32, (1, LANES), 1)
    low = lane < HEAD_DIM
    mine = (lane // HEAD_DIM) == par
    t = q0 + (lax.broadcasted_iota(jnp.int32, (R, 1), 0) & (Q_BLOCK - 1))

    qb = q_ref[...]
    in_low, in_mine = [], []
    for g in range(GQA_GROUP):
        col = qb[:, LANES * (g // 2):LANES * (g // 2 + 1)]
        swapped = pltpu.roll(col, HEAD_DIM, 1)
        in_low.append(col if g % 2 == 0 else swapped)
        in_mine.append(jnp.where((g % 2) == par, col, swapped))
    slope_cols = slope_ref[...]
    q_low = jnp.where(low, jnp.concatenate(in_low, axis=0), slope_cols)
    q_par = jnp.where(mine, jnp.concatenate(in_mine, axis=0), slope_cols)

    nc = kc_ref.shape[0]
    c_ix = lax.broadcasted_iota(jnp.int32, (1, nc), 1)
    mask_c = (t >= c_ix * CMP_STRIDE + (CMP_LEN - 1)) & (c_ix < n_cmp)
    s_c = jnp.where(mask_c, _dot_nt(q_low, kc_ref[...]), NEG)
    e_c = jnp.exp(s_c - jnp.max(s_c, axis=-1, keepdims=True))
    d_c = jnp.sum(e_c, axis=-1, keepdims=True)
    p_c = e_c * jnp.where(t >= CMP_LEN - 1, 1.0 / d_c, 0.0)
    o_c = _dot(p_c.astype(BF16), vc_ref[...])

    pc_sum = p_c[0:Q_BLOCK]
    for g in range(1, GQA_GROUP):
        pc_sum = pc_sum + p_c[g * Q_BLOCK:(g + 1) * Q_BLOCK]
    ovl = ovl_ref[...]
    imp = sum(_dot_nt(ovl, part) for part in _split3(pc_sum))
    j_ix = lax.broadcasted_iota(jnp.int32, (n_sel, 1), 0)
    j_f = j_ix.astype(F32)
    blk_t = (q0 + lax.broadcasted_iota(jnp.int32, (1, Q_BLOCK), 1)) // SEL_LEN
    forced = (j_ix == 0) | (j_ix == blk_t) | (j_ix == blk_t - 1)
    score = jnp.where((j_ix <= blk_t) & ~forced, imp, -FORCE_SCORE)
    sel = forced
    for _ in range(min(SEL_TOPK, n_sel) - N_FORCED):
        m = jnp.max(score, axis=0, keepdims=True)
        first = jnp.min(jnp.where(score == m, j_f, float(n_sel)), axis=0, keepdims=True)
        hit = j_f == first
        sel = sel | (hit & (m > -0.5 * FORCE_SCORE))
        score = jnp.where(hit, TAKEN, score)
    sel_bias = jnp.where(sel, 0.0, NEG).T.astype(BF16)
    q_sel = jnp.concatenate([q_par, jnp.concatenate([sel_bias] * GQA_GROUP, axis=0)], axis=1)

    def k_tile(k_ref, start, size):
        return jnp.where(mine, k_ref[pl.ds(start, size), :], pos_ref[pl.ds(start, size), :])

    def v_tile(v_ref, start, size):
        return jnp.where(mine, v_ref[pl.ds(start, size), :], jnp.ones((), BF16))

    span = WIN + Q_BLOCK
    w0 = pl.multiple_of(jnp.maximum(q0 - WIN, 0), Q_BLOCK)
    dist_w = t - (w0 + lax.broadcasted_iota(jnp.int32, (1, span), 1))
    mask_w = dist_w.astype(jnp.uint32) < WIN
    s_w = jnp.where(mask_w, _dot_nt(q_par, k_tile(kw_ref, w0, span)), NEG)
    e_w = jnp.exp(s_w - jnp.max(s_w, axis=-1, keepdims=True))
    acc_w = _dot(e_w.astype(BF16), v_tile(vw_ref, w0, span))

    def scores(k0):
        k_aug = jnp.concatenate([k_tile(ks_ref, k0, NSA_TK), hot_ref[pl.ds(k0, NSA_TK), :]], axis=1)
        return _dot_nt(q_sel, k_aug)

    def absorb(s, k0, m_run, acc):
        m_new = jnp.maximum(m_run, jnp.max(s, axis=-1, keepdims=True))
        p = jnp.exp((s - m_new).astype(BF16))
        acc = jnp.exp(m_run - m_new) * acc + _dot(p, v_tile(vs_ref, k0, NSA_TK))
        return m_new, acc

    def sweep(kt, carry):
        m_run, acc, s = carry
        k0 = pl.multiple_of(kt * NSA_TK, NSA_TK)
        s_next = scores(k0 + NSA_TK)
        m_run, acc = absorb(s, k0, m_run, acc)
        return m_run, acc, s_next

    n_full = q0 // NSA_TK
    carry = (jnp.full((R, 1), NEG, F32), jnp.zeros((R, LANES), F32), scores(0))
    m_run, acc_s, s_last = lax.fori_loop(0, n_full, sweep, carry)
    k_last = pl.multiple_of(n_full * NSA_TK, NSA_TK)
    s_last = jnp.where(t >= k_last + lax.broadcasted_iota(jnp.int32, (1, NSA_TK), 1), s_last, NEG)
    _, acc_s = absorb(s_last, k_last, m_run, acc_s)

    def normalised(acc):
        return acc / jnp.where(mine, pltpu.roll(acc, HEAD_DIM, 1), 1.0)

    o_sw_all = (normalised(acc_s), normalised(acc_w))
    gate = _sigmoid(gate_ref[...])
    placed = []
    for g in range(GQA_GROUP):
        rows = slice(g * Q_BLOCK, (g + 1) * Q_BLOCK)
        o_sw = (gate[:, GQA_GROUP + g:GQA_GROUP + g + 1] * o_sw_all[0][rows]
                + gate[:, 2 * GQA_GROUP + g:2 * GQA_GROUP + g + 1] * o_sw_all[1][rows])
        o_cg = gate[:, g:g + 1] * o_c[rows]
        if g % 2 == 1:
            o_cg = pltpu.roll(o_cg, HEAD_DIM, 1)
        o_sw = jnp.where((g % 2) == par, o_sw, pltpu.roll(o_sw, HEAD_DIM, 1))
        placed.append(o_cg + o_sw)
    out = [jnp.where(low, placed[2 * c], placed[2 * c + 1]) for c in range(GQA_GROUP // 2)]
    o_ref[...] = jnp.concatenate(out, axis=1).astype(o_ref.dtype)


def _selection_overlap_t(n_cmp_pad, n_cmp, n_sel):
    cs = np.arange(n_cmp_pad) * CMP_STRIDE
    ce = cs + CMP_LEN - 1
    js = np.arange(n_sel) * SEL_LEN
    m = (cs[None, :] <= js[:, None] + SEL_LEN - 1) & (ce[None, :] >= js[:, None])
    m &= (np.arange(n_cmp_pad) < n_cmp)[None, :]
    return m.astype(np.float32)


def _nsa(q, cmp_kv, kv, gates, B, S):
    n_sel = S // SEL_LEN
    assert n_sel >= SEL_TOPK and NSA_TK % Q_BLOCK == 0 and S % NSA_TK == 0 and S >= WIN + Q_BLOCK
    n_cmp = (S - CMP_LEN) // CMP_STRIDE + 1
    nc = S // CMP_STRIDE
    nQ = S // Q_BLOCK
    T = B * S
    ovl = jnp.asarray(_selection_overlap_t(nc, n_cmp, n_sel), BF16)
    pos = np.arange(S)
    hot = jnp.asarray((pos[:, None] // SEL_LEN) == np.arange(n_sel)[None, :], BF16)
    pos_cols = jnp.asarray(_pos_columns((pos // 64) * 64.0, (pos % 64) * 1.0), BF16)
    cmp_spec = lambda a: pl.BlockSpec((None, None, None, nc, LANES), lambda bb, h, i: (a, bb, h, 0, 0))
    kv_spec = lambda a: pl.BlockSpec((S, LANES), lambda bb, h, i: (bb, 2 * a + h // 2))
    const = lambda shape: pl.BlockSpec(shape, lambda bb, h, i: (0,) * len(shape))
    return pl.pallas_call(
        functools.partial(_nsa_body, n_sel=n_sel, n_cmp=n_cmp),
        grid=(B, N_KV_HEADS, nQ),
        in_specs=[pl.BlockSpec((Q_BLOCK, GQA_GROUP * HEAD_DIM), lambda bb, h, i: (bb * nQ + i, h)),
                  cmp_spec(0), cmp_spec(1), kv_spec(2), kv_spec(3), kv_spec(4), kv_spec(5),
                  pl.BlockSpec((None, None, Q_BLOCK, 3 * GQA_GROUP), lambda bb, h, i: (bb, h, i, 0)),
                  const((n_sel, nc)), const((S, n_sel)), const((S, LANES)),
                  pl.BlockSpec((None, GQA_GROUP * Q_BLOCK, LANES), lambda bb, h, i: (h, 0, 0))],
        out_specs=pl.BlockSpec((Q_BLOCK, GQA_GROUP * HEAD_DIM), lambda bb, h, i: (bb * nQ + i, h)),
        out_shape=jax.ShapeDtypeStruct((T, N_HEADS * HEAD_DIM), BF16),
        compiler_params=_cparams(("parallel", "parallel", "arbitrary")),
        name="nsa",
    )(q, cmp_kv, cmp_kv, kv, kv, kv, kv, gates, ovl, hot, pos_cols, _slope_columns())


def _memkv_body(m_ref, g_ref, wk_ref, wv_ref, k_ref, v_ref):
    h = _rms(m_ref[...], g_ref[...]).astype(BF16)
    k_ref[...] = _dot(h, wk_ref[...]).astype(BF16)
    v_ref[...] = _dot(h, wv_ref[...]).astype(BF16)


def _memkv(mem2d, g, wk, wv, tm=256):
    R = mem2d.shape[0]
    wspec = pl.BlockSpec((D_MODEL, XW), lambda i: (0, 0))
    ospec = pl.BlockSpec((tm, XW), lambda i: (i, 0))
    return pl.pallas_call(
        _memkv_body,
        grid=(R // tm,),
        in_specs=[pl.BlockSpec((tm, D_MODEL), lambda i: (i, 0)), pl.BlockSpec((1, D_MODEL), lambda i: (0, 0)),
                  wspec, wspec],
        out_specs=[ospec, ospec],
        out_shape=[jax.ShapeDtypeStruct((R, XW), BF16)] * 2,
        compiler_params=_cparams(("parallel",)),
        name="memkv",
    )(mem2d, g, wk, wv)


def _mid_body(x_ref, yc_ref, yn_ref, wo_ref, gx_ref, wq_ref, mk_ref, mv_ref, wxo_ref, gf_ref, wr_ref, br_ref,
              x2_ref, h3_ref, idx_ref, gate_ref):
    x1 = x_ref[...] + _dot(yc_ref[...], wo_ref[0:CONV_CH, :]) + _dot(yn_ref[...], wo_ref[CONV_CH:2 * CONV_CH, :])
    h = _rms(x1, gx_ref[...]).astype(BF16)
    q = (_dot(h, wq_ref[...]) * (XHEAD_DIM ** -0.5)).astype(BF16)
    heads = []
    for hd in range(N_XHEADS):
        cols = slice(hd * XHEAD_DIM, (hd + 1) * XHEAD_DIM)
        s = _dot_nt(q[:, cols], mk_ref[:, cols])
        e = jnp.exp(s - jnp.max(s, axis=-1, keepdims=True))
        o = _dot(e.astype(BF16), mv_ref[:, cols]) / jnp.sum(e, axis=-1, keepdims=True)
        heads.append(o.astype(BF16))
    x2 = x1 + _dot(jnp.concatenate(heads, axis=-1), wxo_ref[...])
    x2_ref[...] = x2
    h3 = _rms(x2, gf_ref[...])
    h3_ref[...] = h3
    h_hi, h_mid, _ = _split3(h3)
    w_hi, w_mid, _ = _split3(wr_ref[...])
    logits = _dot(h_hi, w_hi) + _dot(h_hi, w_mid) + _dot(h_mid, w_hi) + br_ref[...]
    lane = lax.broadcasted_iota(jnp.int32, (1, LANES), 1)
    lane_f = lane.astype(F32)
    idx_out = jnp.zeros(logits.shape, F32)
    val_out = jnp.full(logits.shape, NEG, F32)
    for k in range(TOP_K):
        m = jnp.max(logits, axis=-1, keepdims=True)
        first = jnp.min(jnp.where(logits == m, lane_f, float(LANES)), axis=-1, keepdims=True)
        idx_out = jnp.where(lane == k, first, idx_out)
        val_out = jnp.where(lane == k, m, val_out)
        logits = jnp.where(lane_f == first, TAKEN, logits)
    e = jnp.where(lane < TOP_K, jnp.exp(val_out - jnp.max(val_out, axis=-1, keepdims=True)), 0.0)
    idx_ref[...] = idx_out.astype(jnp.int32)
    gate_ref[...] = e / jnp.sum(e, axis=-1, keepdims=True)


def _mid(x2d, yc, yn, w_out, g_x, w_xq, mk, mv, w_xo, g_ffn, w_r, b_r, B, S, tm=256):
    T = x2d.shape[0]
    nT = S // tm
    M = mk.shape[0] // B
    row = lambda w: pl.BlockSpec((tm, w), lambda i: (i, 0))
    full = lambda a: pl.BlockSpec(a.shape, lambda i: (0,) * a.ndim)
    memspec = pl.BlockSpec((M, XW), lambda i: (i // nT, 0))
    return pl.pallas_call(
        _mid_body,
        grid=(T // tm,),
        in_specs=[row(D_MODEL), row(CONV_CH), row(CONV_CH), full(w_out), full(g_x), full(w_xq), memspec, memspec,
                  full(w_xo), full(g_ffn), full(w_r), full(b_r)],
        out_specs=[row(D_MODEL), row(D_MODEL), row(LANES), row(LANES)],
        out_shape=[jax.ShapeDtypeStruct((T, D_MODEL), F32), jax.ShapeDtypeStruct((T, D_MODEL), F32),
                   jax.ShapeDtypeStruct((T, LANES), jnp.int32), jax.ShapeDtypeStruct((T, LANES), F32)],
        compiler_params=_cparams(("parallel",)),
        name="mid",
    )(x2d, yc, yn, w_out, g_x, w_xq, mk, mv, w_xo, g_ffn, w_r, b_r)


MOE_TM = 512
MOE_TF = 256
MOE_ST = 4


def _gmm_body(se_ref, st0_ref, nrt_ref, sb_ref, nsu_ref, tok_ref, h_hbm, wg_ref, wl_ref, bg_ref, bl_ref, wd_ref,
              bd_ref, y_hbm, stage, hb16, acc_ref, sem_in, sem_out):
    s = pl.program_id(0)
    f = pl.program_id(1)
    nf = pl.num_programs(1)

    def out_copy(src, tile):
        row0 = pl.multiple_of(tile * MOE_TM, MOE_TM)
        return pltpu.make_async_copy(src, y_hbm.at[pl.ds(row0, MOE_TM), :], sem_out)

    def for_tiles(sup_ix, fn):
        def body(r, c):
            fn(r, st0_ref[sup_ix] + r)
            return c
        lax.fori_loop(0, nrt_ref[sup_ix], body, 0)

    @pl.when(s < nsu_ref[0])
    def _():
        n_rt = nrt_ref[s]

        @pl.when(f == 0)
        def _():
            def request(r):
                base = sb_ref[s] + r * MOE_TM
                slot = r % 2

                def issue(j, c2):
                    tok = tok_ref[base + j]
                    pltpu.make_async_copy(h_hbm.at[pl.ds(tok, 1), :], stage.at[slot, pl.ds(j, 1), :],
                                          sem_in.at[slot]).start()
                    return c2
                lax.fori_loop(0, MOE_TM, issue, 0, unroll=8)

            def gather_tile(r, c):
                @pl.when(r + 1 < n_rt)
                def _():
                    request(r + 1)
                slot = r % 2
                pltpu.make_async_copy(h_hbm.at[pl.ds(0, MOE_TM), :], stage.at[slot], sem_in.at[slot]).wait()
                hb16[r] = stage[slot].astype(BF16)
                return c
            request(0)
            lax.fori_loop(0, n_rt, gather_tile, 0)

            @pl.when(s > 0)
            def _():
                for_tiles(s - 1, lambda r, tile: out_copy(acc_ref.at[r], tile).wait())

            def start_from_bias(r, tile):
                acc_ref[r] = jnp.broadcast_to(bd_ref[...], (MOE_TM, D_MODEL))
            for_tiles(s, start_from_bias)

        wg = wg_ref[...].astype(BF16)
        wl = wl_ref[...].astype(BF16)
        wd = wd_ref[...].astype(BF16)

        def row_tile(r, c):
            h = hb16[r]
            x_glu = jnp.minimum(_dot(h, wg) + bg_ref[...], SWIGLU_LIMIT)
            x_lin = jnp.clip(_dot(h, wl) + bl_ref[...], -SWIGLU_LIMIT, SWIGLU_LIMIT)
            act = x_glu * _sigmoid(SWIGLU_ALPHA * x_glu) * (x_lin + 1.0)
            acc_ref[r] += _dot(act.astype(BF16), wd)
            return c
        lax.fori_loop(0, n_rt, row_tile, 0)

        @pl.when(f == nf - 1)
        def _():
            for_tiles(s, lambda r, tile: out_copy(acc_ref.at[r], tile).start())

            @pl.when(s == nsu_ref[0] - 1)
            def _():
                for_tiles(s, lambda r, tile: out_copy(acc_ref.at[r], tile).wait())
                stage[0] = jnp.zeros((MOE_TM, D_MODEL), F32)

                def fill(tile, c):
                    cp = out_copy(stage.at[0], tile)
                    cp.start()
                    cp.wait()
                    return c
                lax.fori_loop(st0_ref[s] + n_rt, y_hbm.shape[0] // MOE_TM, fill, 0)


def _gmm(sup, sorted_tok, h3, w_up, b_up, w_down, b_down, n_rows):
    se, st0, nrt, sb, nsu = sup
    nf = D_FF // MOE_TF

    def live(s, f, se_, nsu_):
        ok = s < nsu_[0]
        return jnp.where(ok, se_[s], se_[jnp.maximum(nsu_[0] - 1, 0)]), jnp.where(ok, f, nf - 1)

    def wg_map(s, f, se_, st0_, nrt_, sb_, nsu_, tok_):
        e, ff = live(s, f, se_, nsu_)
        return (e, 0, ff)

    def wl_map(s, f, se_, st0_, nrt_, sb_, nsu_, tok_):
        e, ff = live(s, f, se_, nsu_)
        return (e, 0, nf + ff)

    def wd_map(s, f, se_, st0_, nrt_, sb_, nsu_, tok_):
        e, ff = live(s, f, se_, nsu_)
        return (e, ff, 0)

    def bd_map(s, f, se_, st0_, nrt_, sb_, nsu_, tok_):
        e, _ = live(s, f, se_, nsu_)
        return (e, 0, 0)

    grid_spec = pltpu.PrefetchScalarGridSpec(
        num_scalar_prefetch=6,
        grid=(se.shape[0], nf),
        in_specs=[pl.BlockSpec(memory_space=pl.ANY),
                  pl.BlockSpec((None, D_MODEL, MOE_TF), wg_map),
                  pl.BlockSpec((None, D_MODEL, MOE_TF), wl_map),
                  pl.BlockSpec((None, 1, MOE_TF), wg_map),
                  pl.BlockSpec((None, 1, MOE_TF), wl_map),
                  pl.BlockSpec((None, MOE_TF, D_MODEL), wd_map),
                  pl.BlockSpec((None, 1, D_MODEL), bd_map)],
        out_specs=pl.BlockSpec(memory_space=pl.ANY),
        scratch_shapes=[pltpu.VMEM((2, MOE_TM, D_MODEL), F32), pltpu.VMEM((MOE_ST, MOE_TM, D_MODEL), BF16),
                        pltpu.VMEM((MOE_ST, MOE_TM, D_MODEL), F32), pltpu.SemaphoreType.DMA((2,)),
                        pltpu.SemaphoreType.DMA(())],
    )
    return pl.pallas_call(
        _gmm_body,
        grid_spec=grid_spec,
        out_shape=jax.ShapeDtypeStruct((n_rows, D_MODEL), F32),
        compiler_params=_cparams(("arbitrary", "arbitrary")),
        name="moe_gmm",
    )(se, st0, nrt, sb, nsu, sorted_tok, h3, w_up, w_up, b_up, b_up, w_down, b_down)


CMB_TT = 128


def _combine_body(pos_ref, y_hbm, gate_ref, x_ref, g_ref, o_ref, buf, sem):
    i = pl.program_id(0)
    n = pl.num_programs(0)

    def request(tile):
        slot = tile % 2

        def issue(r, c):
            for k in range(TOP_K):
                p = pos_ref[(tile * CMB_TT + r) * TOP_K + k]
                pltpu.make_async_copy(y_hbm.at[pl.ds(p, 1), :], buf.at[slot, k, pl.ds(r, 1), :], sem.at[slot]).start()
            return c
        lax.fori_loop(0, CMB_TT, issue, 0, unroll=4)

    @pl.when(i == 0)
    def _():
        request(i)

    @pl.when(i + 1 < n)
    def _():
        request(i + 1)

    slot = i % 2
    for k in range(TOP_K):
        pltpu.make_async_copy(y_hbm.at[pl.ds(0, CMB_TT), :], buf.at[slot, k], sem.at[slot]).wait()
    gate = gate_ref[...]
    y = x_ref[...]
    for k in range(TOP_K):
        y = y + gate[:, k:k + 1] * buf[slot, k]
    o_ref[...] = _rms(y, g_ref[...])


def _combine(pos, y_rows, gate, x2, g_final):
    T = x2.shape[0]
    grid_spec = pltpu.PrefetchScalarGridSpec(
        num_scalar_prefetch=1,
        grid=(T // CMB_TT,),
        in_specs=[pl.BlockSpec(memory_space=pl.ANY),
                  pl.BlockSpec((CMB_TT, LANES), lambda i, p: (i, 0)),
                  pl.BlockSpec((CMB_TT, D_MODEL), lambda i, p: (i, 0)),
                  pl.BlockSpec((1, D_MODEL), lambda i, p: (0, 0))],
        out_specs=pl.BlockSpec((CMB_TT, D_MODEL), lambda i, p: (i, 0)),
        scratch_shapes=[pltpu.VMEM((2, TOP_K, CMB_TT, D_MODEL), F32), pltpu.SemaphoreType.DMA((2,))],
    )
    return pl.pallas_call(
        _combine_body,
        grid_spec=grid_spec,
        out_shape=jax.ShapeDtypeStruct((T, D_MODEL), F32),
        compiler_params=_cparams(("arbitrary",)),
        name="moe_combine",
    )(pos, y_rows, gate, x2, g_final)


def _route(top_idx):
    T = top_idx.shape[0]
    n_assign = T * TOP_K
    flat_e = top_idx.reshape(-1)
    order = jnp.argsort(flat_e, stable=True)
    onehot = (flat_e[:, None] == jnp.arange(N_EXPERTS, dtype=jnp.int32)[None, :]).astype(jnp.int32)
    running = jnp.cumsum(onehot, axis=0)
    counts = running[-1]
    within = jnp.sum(onehot * (running - 1), axis=1)
    padded = (counts + MOE_TM - 1) // MOE_TM * MOE_TM
    start = jnp.cumsum(counts) - counts
    pend = jnp.cumsum(padded)
    pstart = pend - padded
    n_tiles = -(-n_assign // MOE_TM) + N_EXPERTS
    tiles_e = padded // MOE_TM
    supers_e = (tiles_e + MOE_ST - 1) // MOE_ST
    s_end = jnp.cumsum(supers_e)
    s_ix = jnp.arange(-(-n_tiles // MOE_ST) + N_EXPERTS, dtype=jnp.int32)
    s_e = jnp.minimum(jnp.sum(s_ix[:, None] >= s_end[None, :], axis=1), N_EXPERTS - 1).astype(jnp.int32)
    local = s_ix - (s_end - supers_e)[s_e]
    i32 = lambda v: v.astype(jnp.int32)
    sup = (s_e, i32(pstart[s_e] // MOE_TM + MOE_ST * local), i32(jnp.clip(tiles_e[s_e] - MOE_ST * local, 0, MOE_ST)),
           i32(start[s_e] + MOE_ST * MOE_TM * local), i32(s_end[-1]).reshape(1))
    sorted_tok = jnp.pad((order // TOP_K).astype(jnp.int32), (0, MOE_TM))
    pos = (pstart[flat_e] + within).astype(jnp.int32)
    return sup, sorted_tok, pos, n_tiles * MOE_TM


def _layer(x, mem, g_mix, w_in, conv_w, conv_b, conv_ln_g, conv_ln_b, pe_cmp, w_cmp1, b_cmp1, w_cmp2, b_cmp2,
           w_out, g_x, g_mem, w_xq, w_xk, w_xv, w_xo, g_ffn, w_router, b_router, w_up, b_up, w_down, b_down):
    B, S, D = x.shape
    T = B * S
    x2d = x.reshape(T, D)
    row = lambda v: v.reshape(1, -1)

    w_pad = jnp.pad(w_in, ((0, 0), (0, N_IN_PAD - w_in.shape[1]))).astype(BF16)
    uc, q, kv, gt = _inproj(x2d, row(g_mix), w_pad)

    y_conv = _conv(uc, B, S, conv_w, row(conv_b), row(conv_ln_g), row(conv_ln_b))

    kv16 = kv[:, :2 * KV_WIDTH].reshape(B, S, 2, N_KV_HEADS, HEAD_DIM).transpose(2, 0, 3, 1, 4) \
        .reshape(2, B, N_KV_HEADS, S // CMP_STRIDE, CMP_STRIDE * HEAD_DIM)
    cmp_kv = _compress(kv16, pe_cmp.reshape(2, 1, CMP_LEN * HEAD_DIM), w_cmp1.astype(BF16),
                       b_cmp1.reshape(2, 1, CMP_HIDDEN), w_cmp2.astype(BF16), b_cmp2.reshape(2, 1, HEAD_DIM))
    gates = gt[:, :N_GATE].reshape(B, S, 3, N_KV_HEADS, GQA_GROUP).transpose(0, 3, 1, 2, 4) \
        .reshape(B, N_KV_HEADS, S, 3 * GQA_GROUP)
    y_nsa = _nsa(q, cmp_kv, kv, gates, B, S)

    M = mem.shape[1]
    mk, mv = _memkv(mem.reshape(B * M, D), row(g_mem), w_xk.astype(BF16), w_xv.astype(BF16))
    w_r = jnp.pad(w_router, ((0, 0), (0, LANES - N_EXPERTS)))
    b_r = jnp.pad(b_router, (0, LANES - N_EXPERTS), constant_values=NEG).reshape(1, LANES)
    x2, h3, idx, gate = _mid(x2d, y_conv, y_nsa, w_out.astype(BF16), row(g_x), w_xq.astype(BF16), mk, mv,
                             w_xo.astype(BF16), row(g_ffn), w_r, b_r, B, S)

    sup, sorted_tok, pos, n_rows = _route(idx[:, :TOP_K])
    y_rows = _gmm(sup, sorted_tok, h3, w_up, b_up.reshape(N_EXPERTS, 1, 2 * D_FF), w_down,
                  b_down.reshape(N_EXPERTS, 1, D_MODEL), n_rows)
    return x2, y_rows, pos, gate


def kernel(x, mem, g_mix, w_in, conv_w, conv_b, conv_ln_g, conv_ln_b, pe_cmp, w_cmp1, b_cmp1, w_cmp2, b_cmp2, w_out, g_x, g_mem, w_xq, w_xk, w_xv, w_xo, g_ffn, w_router, b_router, w_up, b_up, w_down, b_down, g_final):
    B, S, D = x.shape
    assert g_mix.shape[0] == 1, "single-layer block"
    x2, y_rows, pos, gate = _layer(x, mem, g_mix[0], w_in[0], conv_w[0], conv_b[0], conv_ln_g[0], conv_ln_b[0],
                                   pe_cmp[0], w_cmp1[0], b_cmp1[0], w_cmp2[0], b_cmp2[0], w_out[0], g_x[0],
                                   g_mem[0], w_xq[0], w_xk[0], w_xv[0], w_xo[0], g_ffn[0], w_router[0],
                                   b_router[0], w_up[0], b_up[0], w_down[0], b_down[0])
    out = _combine(pos, y_rows, gate, x2, g_final.reshape(1, D))
    return out.reshape(B, S, D)
```
